```python
import math
import jax
import jax.numpy as jnp
from jax import lax
import numpy as np

D_MODEL = 1024
BATCH = 16
SEQ = 2048
DEPTH = 2
DEC_BATCH = 32
DEC_SEQ = 1
PAST_LEN = 16384
PAGE_SIZE = 128

N_EVEN = (DEPTH + 1) // 2
N_ODD = DEPTH // 2
DA_HEADS = 4
DA_DH = 64
DA_VD = 2 * DA_DH
GLA_HEADS = 4
GLA_DK = 64
GLA_DV = 128
GLA_RANK = 16
GLA_TAU = 16.0
GLA_CHUNK = 16
DA_QK = DA_HEADS * 2 * DA_DH
DA_V = DA_HEADS * DA_VD
GLA_QK = GLA_HEADS * GLA_DK
GLA_V = GLA_HEADS * GLA_DV
SPLIT_SIZES = (DA_QK, DA_QK, DA_V, GLA_QK, GLA_QK, GLA_V, GLA_RANK, GLA_V)
D_IN_EVEN = sum(SPLIT_SIZES)
D_MIX = DA_V + GLA_V
D_RNN = D_MODEL
RNN_BLOCKS = 8
RNN_BW = D_RNN // RNN_BLOCKS
CONV_W = 4
RG_C = 8.0
D_FF = 4 * D_MODEL
NUM_BUCKETS = 32
MAX_DISTANCE = 128
Q_BLOCK = 128
EPS = 1e-6
NEG = -1e30

kernel_name = 'hybrid_diffattn_gla_rglru_decode_step'


def rmsnorm(x, g):
    xf = x.astype(jnp.float32)
    y = xf * lax.rsqrt(jnp.mean(xf * xf, axis=-1, keepdims=True) + EPS)
    return (y * g.astype(jnp.float32)).astype(x.dtype)


def t5_bucket(rel):
    rel = jnp.maximum(rel, 0)
    max_exact = NUM_BUCKETS // 2
    large = max_exact + (jnp.log(jnp.maximum(rel, 1).astype(jnp.float32) / max_exact)
                         / math.log(MAX_DISTANCE / max_exact) * (NUM_BUCKETS - max_exact)).astype(jnp.int32)
    large = jnp.minimum(large, NUM_BUCKETS - 1)
    return jnp.where(rel < max_exact, rel, large)


def rel_bias(table, qpos, kpos):
    bucket = t5_bucket(qpos[:, None] - kpos[None, :])
    return jnp.transpose(table[bucket], (2, 0, 1)).astype(jnp.float32)


def diff_lambda(lq1, lk1, lq2, lk2, lam_init):
    f = jnp.float32
    return (jnp.exp(jnp.sum(lq1.astype(f) * lk1.astype(f)))
            - jnp.exp(jnp.sum(lq2.astype(f) * lk2.astype(f))) + lam_init)


def diff_probs(l1, l2, lam):
    return jax.nn.softmax(l1, axis=-1) - lam * jax.nn.softmax(l2, axis=-1)


def diff_attn_prompt(q, k, v, lam, table):
    B, T = q.shape[:2]
    q1, q2 = jnp.split(q, 2, axis=-1)
    k1, k2 = jnp.split(k, 2, axis=-1)
    kpos = jnp.arange(T)
    scale = DA_DH ** -0.5

    def block(i):
        start = i * Q_BLOCK
        qb1 = lax.dynamic_slice_in_dim(q1, start, Q_BLOCK, axis=1)
        qb2 = lax.dynamic_slice_in_dim(q2, start, Q_BLOCK, axis=1)
        qpos = start + jnp.arange(Q_BLOCK)
        bias = rel_bias(table, qpos, kpos)
        mask = qpos[:, None] >= kpos[None, :]
        l1 = jnp.where(mask, jnp.einsum('bqhd,bkhd->bhqk', qb1, k1).astype(jnp.float32) * scale + bias, NEG)
        l2 = jnp.where(mask, jnp.einsum('bqhd,bkhd->bhqk', qb2, k2).astype(jnp.float32) * scale + bias, NEG)
        a = diff_probs(l1, l2, lam)
        return jnp.einsum('bhqk,bkhd->bqhd', a, v.astype(jnp.float32))

    out = lax.map(block, jnp.arange(T // Q_BLOCK))
    return jnp.moveaxis(out, 0, 1).reshape(B, T, DA_HEADS, DA_VD).astype(q.dtype)


def diff_attn_sample(q, k_new, v_new, k_past, v_past, lam, table):
    T = q.shape[1]
    P = k_past.shape[1]
    q1, q2 = jnp.split(q, 2, axis=-1)
    kn1, kn2 = jnp.split(k_new, 2, axis=-1)
    kp1, kp2 = jnp.split(k_past, 2, axis=-1)
    scale = DA_DH ** -0.5
    qpos = P + jnp.arange(T)
    bias_p = rel_bias(table, qpos, jnp.arange(P))
    bias_n = rel_bias(table, qpos, qpos)
    mask_n = qpos[:, None] >= qpos[None, :]

    def logits(qh, kp, kn):
        lp = jnp.einsum('bqhd,bkhd->bhqk', qh, kp).astype(jnp.float32) * scale + bias_p
        ln = jnp.where(mask_n, jnp.einsum('bqhd,bkhd->bhqk', qh, kn).astype(jnp.float32) * scale + bias_n, NEG)
        return jnp.concatenate([lp, ln], axis=-1)

    a = diff_probs(logits(q1, kp1, kn1), logits(q2, kp2, kn2), lam)
    out = (jnp.einsum('bhqk,bkhd->bqhd', a[..., :P], v_past.astype(jnp.float32))
           + jnp.einsum('bhqk,bkhd->bqhd', a[..., P:], v_new.astype(jnp.float32)))
    return out.astype(q.dtype)


def gla_recurrence(q, k, v, log_a, s0):
    B, T, H, DK = q.shape
    DV = v.shape[-1]
    C = GLA_CHUNK if T % GLA_CHUNK == 0 else 1
    nc = T // C
    f = jnp.float32

    def chunks(t):
        return jnp.moveaxis(t.reshape(B, nc, C, *t.shape[2:]), 1, 0)

    bcum = jnp.cumsum(chunks(log_a), axis=2)
    causal = jnp.tril(jnp.ones((C, C), dtype=bool))

    def step(S, inp):
        qc, kc, vc, bc = inp
        qe = qc.astype(f) * jnp.exp(bc)
        ke = kc.astype(f) * jnp.exp(-bc)
        att = jnp.where(causal, jnp.einsum('bchk,bshk->bhcs', qe, ke), 0.0)
        o = jnp.einsum('bhcs,bshv->bchv', att, vc.astype(f)) + jnp.einsum('bchk,bhkv->bchv', qe, S)
        b_last = bc[:, -1]
        kd = kc.astype(f) * jnp.exp(b_last[:, None] - bc)
        S = jnp.exp(b_last)[..., None] * S + jnp.einsum('bshk,bshv->bhkv', kd, vc.astype(f))
        return S, o

    S, o = lax.scan(step, s0.astype(f), (chunks(q), chunks(k), chunks(v), bcum))
    return jnp.moveaxis(o, 0, 1).reshape(B, T, H, DV), S


def even_mixer(h, w_in, w_a2, b_a2, g_da, g_gla, w_out, lam_init, gla_s0, attend):
    B, T, _ = h.shape
    offs = [int(o) for o in np.cumsum(SPLIT_SIZES)[:-1]]
    da_q, da_k, da_v, g_q, g_k, g_v, g_lr, g_og = jnp.split(h @ w_in, offs, axis=-1)
    da_q = da_q.reshape(B, T, DA_HEADS, 2 * DA_DH)
    da_k = da_k.reshape(B, T, DA_HEADS, 2 * DA_DH)
    da_v = da_v.reshape(B, T, DA_HEADS, DA_VD)
    a_out = rmsnorm(attend(da_q, da_k, da_v), g_da) * (1.0 - lam_init)
    log_a = jax.nn.log_sigmoid((g_lr @ w_a2 + b_a2).astype(jnp.float32)) / GLA_TAU
    gq = g_q.reshape(B, T, GLA_HEADS, GLA_DK) * (GLA_DK ** -0.5)
    o, s_new = gla_recurrence(gq, g_k.reshape(B, T, GLA_HEADS, GLA_DK), g_v.reshape(B, T, GLA_HEADS, GLA_DV),
                              log_a.reshape(B, T, GLA_HEADS, GLA_DK), gla_s0)
    o = rmsnorm(o.astype(h.dtype), g_gla) * jax.nn.silu(g_og).reshape(B, T, GLA_HEADS, GLA_DV)
    mix = jnp.concatenate([a_out.reshape(B, T, DA_V), o.reshape(B, T, GLA_V)], axis=-1)
    return (mix @ w_out).astype(h.dtype), da_k, da_v, s_new.astype(h.dtype)


def linear_scan(a, b, h0):
    b = b.at[:, 0].add(a[:, 0] * h0.astype(jnp.float32))

    def combine(l, r):
        al, bl = l
        ar, br = r
        return al * ar, ar * bl + br

    _, hs = lax.associative_scan(combine, (a, b), axis=1)
    return hs


def odd_mixer(h, w_in, conv_w, conv_b, w_ga, b_ga, w_gx, b_gx, lam_rg, w_out, conv_buf, h0):
    B, T, _ = h.shape
    gate_br, x_br = jnp.split(h @ w_in, 2, axis=-1)
    xp = jnp.concatenate([conv_buf.astype(x_br.dtype), x_br], axis=1)
    xc = conv_b + sum(conv_w[j] * xp[:, j:j + T] for j in range(CONV_W))
    new_buf = xp[:, T:]
    xb = xc.reshape(B, T, RNN_BLOCKS, RNN_BW)
    r = jax.nn.sigmoid((jnp.einsum('btnd,nde->btne', xb, w_ga).reshape(B, T, D_RNN) + b_ga).astype(jnp.float32))
    i = jax.nn.sigmoid((jnp.einsum('btnd,nde->btne', xb, w_gx).reshape(B, T, D_RNN) + b_gx).astype(jnp.float32))
    log_a = -RG_C * jax.nn.softplus(-lam_rg.astype(jnp.float32)) * r
    a = jnp.exp(log_a)
    b = jnp.sqrt(-jnp.expm1(2.0 * log_a)) * (i * xc.astype(jnp.float32))
    hs = linear_scan(a, b, h0)
    y = jax.nn.gelu(gate_br) * hs.astype(h.dtype)
    return (y @ w_out).astype(h.dtype), new_buf, hs[:, -1].astype(h.dtype)


def sq_relu_mlp(h, w_up, w_down):
    return jnp.square(jax.nn.relu(h @ w_up)) @ w_down


def setup_inputs(seed: int = 0) -> dict:
    key = jax.random.key(seed)
    ks = list(jax.random.split(key, 40))
    f32 = jnp.float32

    def nrm(shape, scale):
        return jax.random.normal(ks.pop(), shape, f32) * scale

    n_pages = PAST_LEN // PAGE_SIZE
    n_used = DEC_BATCH * n_pages
    n_pool = n_used + n_used // 4
    page_table = jax.random.permutation(ks.pop(), n_pool)[:n_used].reshape(DEC_BATCH, n_pages).astype(jnp.int32)
    a0 = jax.random.uniform(ks.pop(), (N_ODD, D_RNN), f32, 0.9, 0.999)
    s = a0 ** (1.0 / RG_C)
    rg_lambda = jnp.log(s) - jnp.log1p(-s)
    return {
        'x_prompt': nrm((BATCH, SEQ, D_MODEL), 1.0),
        'x_sample': nrm((DEC_BATCH, DEC_SEQ, D_MODEL), 1.0),
        'cache_k': nrm((N_EVEN, n_pool, PAGE_SIZE, DA_HEADS, 2 * DA_DH), 1.0),
        'cache_v': nrm((N_EVEN, n_pool, PAGE_SIZE, DA_HEADS, DA_VD), 1.0),
        'state_gla': nrm((N_EVEN, DEC_BATCH, GLA_HEADS, GLA_DK, GLA_DV), 0.5),
        'state_conv': nrm((N_ODD, DEC_BATCH, CONV_W - 1, D_RNN), 1.0),
        'state_rglru': nrm((N_ODD, DEC_BATCH, D_RNN), 1.0),
        'page_table': page_table,
        'rel_table': nrm((NUM_BUCKETS, DA_HEADS), 0.5),
        'norm_g': 1.0 + nrm((DEPTH, 4, D_MODEL), 0.05),
        'w_up': nrm((DEPTH, D_MODEL, D_FF), D_MODEL ** -0.5),
        'w_down': nrm((DEPTH, D_FF, D_MODEL), D_FF ** -0.5),
        'w_in_even': nrm((N_EVEN, D_MODEL, D_IN_EVEN), D_MODEL ** -0.5),
        'w_alpha2': nrm((N_EVEN, GLA_RANK, GLA_QK), GLA_RANK ** -0.5),
        'b_alpha2': nrm((N_EVEN, GLA_QK), 0.5),
        'lam_q1': nrm((N_EVEN, DA_DH), 0.1),
        'lam_k1': nrm((N_EVEN, DA_DH), 0.1),
        'lam_q2': nrm((N_EVEN, DA_DH), 0.1),
        'lam_k2': nrm((N_EVEN, DA_DH), 0.1),
        'g_diff': 1.0 + nrm((N_EVEN, DA_VD), 0.05),
        'g_gla': 1.0 + nrm((N_EVEN, GLA_DV), 0.05),
        'w_out_even': nrm((N_EVEN, D_MIX, D_MODEL), D_MIX ** -0.5),
        'w_in_odd': nrm((N_ODD, D_MODEL, 2 * D_RNN), D_MODEL ** -0.5),
        'conv_w': nrm((N_ODD, CONV_W, D_RNN), CONV_W ** -0.5),
        'conv_b': nrm((N_ODD, D_RNN), 0.02),
        'w_gate_a': nrm((N_ODD, RNN_BLOCKS, RNN_BW, RNN_BW), RNN_BW ** -0.5),
        'b_gate_a': nrm((N_ODD, D_RNN), 0.02),
        'w_gate_x': nrm((N_ODD, RNN_BLOCKS, RNN_BW, RNN_BW), RNN_BW ** -0.5),
        'b_gate_x': nrm((N_ODD, D_RNN), 0.02),
        'rg_lambda': rg_lambda,
        'w_out_odd': nrm((N_ODD, D_RNN, D_MODEL), D_RNN ** -0.5),
    }


def reference(x_prompt, x_sample, cache_k, cache_v, state_gla, state_conv, state_rglru, page_table,
              rel_table, norm_g, w_up, w_down, w_in_even, w_alpha2, b_alpha2, lam_q1, lam_k1, lam_q2, lam_k2,
              g_diff, g_gla, w_out_even, w_in_odd, conv_w, conv_b, w_gate_a, b_gate_a, w_gate_x, b_gate_x,
              rg_lambda, w_out_odd):
    Bp = x_prompt.shape[0]
    Bd = x_sample.shape[0]
    P = page_table.shape[1] * PAGE_SIZE
    yp, ys = x_prompt, x_sample
    kp_l, vp_l, ks_l, vs_l, gp_l, gs_l = [], [], [], [], [], []
    cp_l, cs_l, rp_l, rs_l = [], [], [], []
    for layer in range(DEPTH):
        g = norm_g[layer]
        hp = rmsnorm(yp, g[0])
        hs = rmsnorm(ys, g[0])
        if layer % 2 == 0:
            li = layer // 2
            lam_init = 0.8 - 0.6 * math.exp(-0.3 * layer)
            lam = diff_lambda(lam_q1[li], lam_k1[li], lam_q2[li], lam_k2[li], lam_init)
            prm = (w_in_even[li], w_alpha2[li], b_alpha2[li], g_diff[li], g_gla[li], w_out_even[li], lam_init)
            s0_p = jnp.zeros((Bp, GLA_HEADS, GLA_DK, GLA_DV), jnp.float32)
            mp, kpn, vpn, sp = even_mixer(hp, *prm, s0_p,
                                          lambda q, k, v: diff_attn_prompt(q, k, v, lam, rel_table))
            k_past = cache_k[li, page_table].reshape(Bd, P, DA_HEADS, 2 * DA_DH)
            v_past = cache_v[li, page_table].reshape(Bd, P, DA_HEADS, DA_VD)
            ms, ksn, vsn, ss = even_mixer(hs, *prm, state_gla[li],
                                          lambda q, k, v: diff_attn_sample(q, k, v, k_past, v_past, lam, rel_table))
            kp_l.append(kpn); vp_l.append(vpn); ks_l.append(ksn); vs_l.append(vsn)
            gp_l.append(sp); gs_l.append(ss)
        else:
            li = layer // 2
            prm = (w_in_odd[li], conv_w[li], conv_b[li], w_gate_a[li], b_gate_a[li], w_gate_x[li], b_gate_x[li],
                   rg_lambda[li], w_out_odd[li])
            mp, cpn, rpn = odd_mixer(hp, *prm, jnp.zeros((Bp, CONV_W - 1, D_RNN), hp.dtype),
                                     jnp.zeros((Bp, D_RNN), jnp.float32))
            ms, csn, rsn = odd_mixer(hs, *prm, state_conv[li], state_rglru[li])
            cp_l.append(cpn); cs_l.append(csn); rp_l.append(rpn); rs_l.append(rsn)
        yp = yp + rmsnorm(mp, g[1])
        ys = ys + rmsnorm(ms, g[1])
        yp = yp + rmsnorm(sq_relu_mlp(rmsnorm(yp, g[2]), w_up[layer], w_down[layer]), g[3])
        ys = ys + rmsnorm(sq_relu_mlp(rmsnorm(ys, g[2]), w_up[layer], w_down[layer]), g[3])
    k_prompt = jnp.stack(kp_l)
    v_prompt = jnp.stack(vp_l)
    k_sample = jnp.stack(ks_l)
    v_sample = jnp.stack(vs_l)
    gla_prompt = jnp.stack(gp_l)
    gla_sample = jnp.stack(gs_l)
    conv_prompt = jnp.stack(cp_l)
    conv_sample = jnp.stack(cs_l)
    rglru_prompt = jnp.stack(rp_l)
    rglru_sample = jnp.stack(rs_l)
    return (yp, ys, k_prompt, v_prompt, k_sample, v_sample, gla_prompt, gla_sample,
            conv_prompt, conv_sample, rglru_prompt, rglru_sample)
```

```python
import functools
import math

import numpy as np
import jax
import jax.numpy as jnp
from jax import lax
from jax.experimental import pallas as pl
from jax.experimental.pallas import tpu as pltpu

DA_HEADS = 4
DA_DH = 64
DA_VD = 2 * DA_DH
GLA_HEADS = 4
GLA_DK = 64
GLA_DV = 128
GLA_RANK = 16
GLA_TAU = 16.0
DA_QK = DA_HEADS * 2 * DA_DH
DA_V = DA_HEADS * DA_VD
GLA_QK = GLA_HEADS * GLA_DK
GLA_V = GLA_HEADS * GLA_DV
SPLIT_SIZES = (DA_QK, DA_QK, DA_V, GLA_QK, GLA_QK, GLA_V, GLA_RANK, GLA_V)
RNN_BLOCKS = 8
CONV_W = 4
RG_C = 8.0
NUM_BUCKETS = 32
MAX_DISTANCE = 128
PAGE_SIZE = 128
EPS = 1e-6
NEG = -1e30

V7X_LANES = 128
V7X_SUBLANES = 8
V7X_VMEM_LIMIT_BYTES = 56 * 1024 * 1024

F32 = jnp.float32
_MXU_DTYPE = jnp.bfloat16

ROW_TILE = 512
FF_CHUNK = 1024
ATTN_TILE = 256
GLA_CHUNK = 64
RNN_TIME_TILE = 256
PAGES_PER_STEP = 8


def _mx(x):
    return x.astype(_MXU_DTYPE)


def _dot(a, b):
    return jnp.dot(_mx(a), _mx(b), preferred_element_type=F32)


def _dot_nt(a, b):
    return lax.dot_general(_mx(a), _mx(b), (((1,), (1,)), ((), ())), preferred_element_type=F32)


def _dot_tn(a, b):
    return lax.dot_general(_mx(a), _mx(b), (((0,), (0,)), ((), ())), preferred_element_type=F32)


def _rms(x, g):
    return x * lax.rsqrt(jnp.mean(x * x, axis=-1, keepdims=True) + EPS) * g


def _softplus(x):
    return jnp.maximum(x, 0.0) + jnp.log1p(jnp.exp(-jnp.abs(x)))


def _log_sigmoid(x):
    return -_softplus(-x)


def _gelu_tanh(x):
    c = math.sqrt(2.0 / math.pi)
    return 0.5 * x * (1.0 + jnp.tanh(c * (x + 0.044715 * (x * x * x))))


def _params(*sem):
    return pltpu.CompilerParams(dimension_semantics=sem, vmem_limit_bytes=V7X_VMEM_LIMIT_BYTES)


def _const_spec(shape):
    nd = len(shape)
    return pl.BlockSpec(shape, lambda *_: (0,) * nd, pipeline_mode=pl.Buffered(1))


def _row_tile(m):
    return ROW_TILE if m % ROW_TILE == 0 else m


def _inproj_even_kernel(x_ref, g_ref, wq, wk, wv, wgq, wgk, wgv, wlr, wog, wa2, ba2,
                        q_o, k_o, v_o, gq_o, gk_o, gv_o, la_o, og_o):
    h = _mx(_rms(x_ref[...], g_ref[...]))
    q_o[...] = (_dot(h, wq[...]) * (DA_DH ** -0.5)).astype(q_o.dtype)
    k_o[...] = _dot(h, wk[...])
    v_o[...] = _dot(h, wv[...])
    gq_o[...] = _dot(h, wgq[...]) * (GLA_DK ** -0.5)
    gk_o[...] = _dot(h, wgk[...])
    gv_o[...] = _dot(h, wgv[...]).astype(gv_o.dtype)
    z = _dot(_dot(h, wlr[...]), wa2[...]) + ba2[...]
    la_o[...] = _log_sigmoid(z) * (1.0 / GLA_TAU)
    og = _dot(h, wog[...])
    og_o[...] = og * jax.nn.sigmoid(og)


def _inproj_even(x, g, w_parts, w_a2, b_a2):
    m, d = x.shape
    tm = _row_tile(m)
    out_cols = (DA_QK, DA_QK, DA_V, GLA_QK, GLA_QK, GLA_V, GLA_QK, GLA_V)
    out_dtypes = (_MXU_DTYPE, F32, F32, F32, F32, _MXU_DTYPE, F32, F32)
    in_specs = [pl.BlockSpec((tm, d), lambda i: (i, 0)), _const_spec((1, d))]
    in_specs += [_const_spec(w.shape) for w in w_parts]
    in_specs += [_const_spec(w_a2.shape), _const_spec((1, GLA_QK))]
    return pl.pallas_call(
        _inproj_even_kernel,
        grid=(m // tm,),
        in_specs=in_specs,
        out_specs=[pl.BlockSpec((tm, n), lambda i: (i, 0)) for n in out_cols],
        out_shape=[jax.ShapeDtypeStruct((m, n), dt) for n, dt in zip(out_cols, out_dtypes)],
        compiler_params=_params("parallel"),
        name="inproj_even",
    )(x, g.reshape(1, d), *w_parts, w_a2, b_a2.reshape(1, GLA_QK))


def _inproj_odd_kernel(x_ref, g_ref, wg, wx, gate_o, x_o):
    h = _mx(_rms(x_ref[...], g_ref[...]))
    gate_o[...] = _gelu_tanh(_dot(h, wg[...]))
    x_o[...] = _dot(h, wx[...])


def _inproj_odd(x, g, w_gate, w_x):
    m, d = x.shape
    tm = _row_tile(m)
    n = w_gate.shape[1]
    return pl.pallas_call(
        _inproj_odd_kernel,
        grid=(m // tm,),
        in_specs=[pl.BlockSpec((tm, d), lambda i: (i, 0)), _const_spec((1, d)),
                  _const_spec(w_gate.shape), _const_spec(w_x.shape)],
        out_specs=[pl.BlockSpec((tm, n), lambda i: (i, 0))] * 2,
        out_shape=[jax.ShapeDtypeStruct((m, n), F32)] * 2,
        compiler_params=_params("parallel"),
        name="inproj_odd",
    )(x, g.reshape(1, d), w_gate, w_x)


def _post_kernel(*refs, n_mix):
    mix_refs = refs[:n_mix]
    wo_refs = refs[n_mix:2 * n_mix]
    x_ref, g_ref, wup_ref, wdn_ref, y_ref = refs[2 * n_mix:]
    m = _dot(mix_refs[0][...], wo_refs[0][...])
    for mr, wr in zip(mix_refs[1:], wo_refs[1:]):
        m = m + _dot(mr[...], wr[...])
    g = g_ref[...]
    y1 = x_ref[...] + _rms(m, g[1:2])
    h = _mx(_rms(y1, g[2:3]))
    d_ff = wup_ref.shape[1]
    fc = min(FF_CHUNK, d_ff)
    acc = None
    for c in range(d_ff // fc):
        a = jnp.square(jnp.maximum(_dot(h, wup_ref[:, c * fc:(c + 1) * fc]), 0.0))
        part = _dot(a, wdn_ref[c * fc:(c + 1) * fc, :])
        acc = part if acc is None else acc + part
    y_ref[...] = y1 + _rms(acc, g[3:4])


def _post(mix_parts, wo_parts, x, g4, w_up, w_down):
    m, d = x.shape
    tm = _row_tile(m)
    n_mix = len(mix_parts)
    in_specs = [pl.BlockSpec((tm, p.shape[1]), lambda i: (i, 0)) for p in mix_parts]
    in_specs += [_const_spec(w.shape) for w in wo_parts]
    in_specs += [pl.BlockSpec((tm, d), lambda i: (i, 0)), _const_spec(g4.shape),
                 _const_spec(w_up.shape), _const_spec(w_down.shape)]
    return pl.pallas_call(
        functools.partial(_post_kernel, n_mix=n_mix),
        grid=(m // tm,),
        in_specs=in_specs,
        out_specs=pl.BlockSpec((tm, d), lambda i: (i, 0)),
        out_shape=jax.ShapeDtypeStruct((m, d), F32),
        compiler_params=_params("parallel"),
        name="post_mlp",
    )(*mix_parts, *wo_parts, x, g4, w_up, w_down)


def _t5_bucket(rel):
    rel = jnp.maximum(rel, 0)
    max_exact = NUM_BUCKETS // 2
    large = max_exact + (jnp.log(jnp.maximum(rel, 1).astype(F32) / max_exact)
                         / math.log(MAX_DISTANCE / max_exact) * (NUM_BUCKETS - max_exact)).astype(jnp.int32)
    large = jnp.minimum(large, NUM_BUCKETS - 1)
    return jnp.where(rel < max_exact, rel, large)


def _rel_bias(table, qpos, kpos):
    bucket = _t5_bucket(qpos[:, None] - kpos[None, :])
    return jnp.transpose(table[bucket], (2, 0, 1)).astype(F32)


def _check_far_bucket_constant(first, last):
    rel = np.arange(first, last + 1, dtype=np.float64)
    max_exact = NUM_BUCKETS // 2
    steps = np.log(rel / max_exact) / math.log(MAX_DISTANCE / max_exact) * (NUM_BUCKETS - max_exact)
    assert first >= max_exact and np.all(steps >= NUM_BUCKETS - max_exact - 0.5), "far keys must share one bias bucket"


def _lambda_from(lam_ref, lam_init):
    lv = lam_ref[...]
    s1 = jnp.sum(lv[0:1] * lv[1:2], axis=-1, keepdims=True)
    s2 = jnp.sum(lv[2:3] * lv[3:4], axis=-1, keepdims=True)
    return jnp.exp(s1) - jnp.exp(s2) + lam_init


def _attn_prompt_kernel(lam_ref, q_ref, k_ref, v_ref, bias_ref, g_ref, o_ref,
                        kb, vb, m_s, l_s, acc_s, *, t, lam_init):
    i = pl.program_id(2)

    @pl.when(i == 0)
    def _():
        kb[...] = _mx(k_ref[0])
        vb[...] = _mx(v_ref[0])

    q = q_ref[0].astype(F32)
    lane = lax.broadcasted_iota(jnp.int32, q.shape, 1)
    zero = jnp.zeros_like(q)
    qs = _mx(jnp.concatenate([jnp.where(lane < DA_DH, q, zero), jnp.where(lane >= DA_DH, q, zero)], axis=0))

    m_s[...] = jnp.full(m_s.shape, -jnp.inf, F32)
    l_s[...] = jnp.zeros(l_s.shape, F32)
    acc_s[...] = jnp.zeros(acc_s.shape, F32)

    def step(j, bias):
        rows = pl.ds(pl.multiple_of(j * t, t), t)
        s = _dot_nt(qs, kb[rows, :])
        if bias is not None:
            s = s + jnp.concatenate([bias, bias], axis=0)
        m_old = m_s[...]
        m_new = jnp.maximum(m_old, jnp.max(s, axis=-1, keepdims=True))
        alpha = jnp.exp(m_old - m_new)
        p = jnp.exp(s - m_new)
        l_s[...] = alpha * l_s[...] + jnp.sum(p, axis=-1, keepdims=True)
        acc_s[...] = alpha * acc_s[...] + _dot(p, vb[rows, :])
        m_s[...] = m_new

    def far(j, carry):
        step(j, None)
        return carry

    lax.fori_loop(0, jnp.maximum(i - 1, 0), far, 0)

    @pl.when(i > 0)
    def _():
        step(i - 1, bias_ref[0, :, 0:t])

    step(i, bias_ref[0, :, t:2 * t])

    lam = _lambda_from(lam_ref, lam_init)
    an = acc_s[...] / l_s[...]
    o = an[0:t] - lam * an[t:2 * t]
    o_ref[0] = (_rms(o, g_ref[...]) * (1.0 - lam_init)).astype(o_ref.dtype)


def _attn_prompt(q, k, v, lam_vecs, bias, g_da, lam_init):
    b, t_len, _ = q.shape
    t = min(ATTN_TILE, t_len)
    nq = t_len // t
    hd = 2 * DA_DH
    return pl.pallas_call(
        functools.partial(_attn_prompt_kernel, t=t, lam_init=lam_init),
        grid=(b, DA_HEADS, nq),
        in_specs=[
            _const_spec(lam_vecs.shape),
            pl.BlockSpec((1, t, hd), lambda bi, h, i: (bi, i, h)),
            pl.BlockSpec((1, t_len, hd), lambda bi, h, i: (bi, 0, h)),
            pl.BlockSpec((1, t_len, DA_VD), lambda bi, h, i: (bi, 0, h)),
            pl.BlockSpec((1, t, 2 * t), lambda bi, h, i: (h, 0, 0)),
            _const_spec((1, DA_VD)),
        ],
        out_specs=pl.BlockSpec((1, t, DA_VD), lambda bi, h, i: (bi, i, h)),
        out_shape=jax.ShapeDtypeStruct((b, t_len, DA_V), _MXU_DTYPE),
        scratch_shapes=[
            pltpu.VMEM((t_len, hd), _MXU_DTYPE),
            pltpu.VMEM((t_len, DA_VD), _MXU_DTYPE),
            pltpu.VMEM((2 * t, 1), F32),
            pltpu.VMEM((2 * t, 1), F32),
            pltpu.VMEM((2 * t, DA_VD), F32),
        ],
        compiler_params=_params("parallel", "parallel", "arbitrary"),
        name="attn_prompt",
    )(lam_vecs, q, k, v, bias, g_da.reshape(1, DA_VD))


def _prompt_bias(rel_table, t, t_len):
    qpos = t + jnp.arange(t)
    kpos = jnp.arange(2 * t)
    bias = _rel_bias(rel_table, qpos, kpos)
    if t_len > 2 * t:
        _check_far_bucket_constant(t + 1, t_len)
    far = rel_table[NUM_BUCKETS - 1].astype(F32)
    bias = bias - far[:, None, None]
    mask = qpos[:, None] >= kpos[None, :]
    return jnp.where(mask[None], bias, NEG)


def _attn_sample_kernel(pt_ref, lam_ref, q_ref, kn_ref, vn_ref, bias_ref, biasn_ref, g_ref, *rest,
                        n_pg, lam_init):
    del pt_ref
    k_refs = rest[:n_pg]
    v_refs = rest[n_pg:2 * n_pg]
    o_ref, m_s, l_s, acc_s = rest[2 * n_pg:]
    step_i = pl.program_id(1)
    n_steps = pl.num_programs(1)
    rows8 = 2 * DA_HEADS
    width = DA_HEADS * 2 * DA_DH

    @pl.when(step_i == 0)
    def _():
        m_s[...] = jnp.full(m_s.shape, -jnp.inf, F32)
        l_s[...] = jnp.zeros(l_s.shape, F32)
        acc_s[...] = jnp.zeros(acc_s.shape, F32)

    rowi = lax.broadcasted_iota(jnp.int32, (rows8, width), 0)
    lanei = lax.broadcasted_iota(jnp.int32, (rows8, width), 1)
    q8 = jnp.where(lanei // DA_DH == rowi, jnp.broadcast_to(q_ref[0].astype(F32), (rows8, width)), 0.0)

    s = jnp.concatenate([_dot_nt(q8, kr[0]) for kr in k_refs], axis=1) + bias_ref[...]
    m_old = m_s[...]
    m_new = jnp.maximum(m_old, jnp.max(s, axis=-1, keepdims=True))
    alpha = jnp.exp(m_old - m_new)
    p = jnp.exp(s - m_new)
    l_new = alpha * l_s[...] + jnp.sum(p, axis=-1, keepdims=True)
    pv = _dot(p[:, 0:PAGE_SIZE], v_refs[0][0])
    for gi in range(1, n_pg):
        pv = pv + _dot(p[:, gi * PAGE_SIZE:(gi + 1) * PAGE_SIZE], v_refs[gi][0])
    acc_new = alpha * acc_s[...] + pv
    m_s[...] = m_new
    l_s[...] = l_new
    acc_s[...] = acc_new

    @pl.when(step_i == n_steps - 1)
    def _():
        sn = jnp.sum(q8 * kn_ref[0], axis=-1, keepdims=True) + biasn_ref[...]
        m_fin = jnp.maximum(m_new, sn)
        a2 = jnp.exp(m_new - m_fin)
        pn = jnp.exp(sn - m_fin)
        l_fin = a2 * l_new + pn
        acc_fin = a2 * acc_new + pn * vn_ref[0]
        an = acc_fin / l_fin
        same_head = (lanei // DA_VD) == (rowi // 2)
        o1 = jnp.sum(jnp.where(same_head & (rowi % 2 == 0), an, 0.0), axis=0, keepdims=True)
        o2 = jnp.sum(jnp.where(same_head & (rowi % 2 == 1), an, 0.0), axis=0, keepdims=True)
        o = o1 - _lambda_from(lam_ref, lam_init) * o2
        for h in range(DA_HEADS):
            seg = o[:, h * DA_VD:(h + 1) * DA_VD]
            o_ref[0, :, h * DA_VD:(h + 1) * DA_VD] = (_rms(seg, g_ref[...]) * (1.0 - lam_init)).astype(o_ref.dtype)


def _attn_sample(q, k_new, v_new, cache_k, cache_v, layer_idx, page_table, lam_vecs, rel_table, g_da, lam_init):
    bd = q.shape[0]
    n_pages = page_table.shape[1]
    past = n_pages * PAGE_SIZE
    n_pool = cache_k.shape[1]
    width = DA_HEADS * 2 * DA_DH
    n_pg = PAGES_PER_STEP if n_pages % PAGES_PER_STEP == 0 else 1
    n_steps = n_pages // n_pg
    ck = cache_k.reshape(cache_k.shape[0] * n_pool, PAGE_SIZE, width)
    cv = cache_v.reshape(cache_v.shape[0] * n_pool, PAGE_SIZE, DA_V)
    base = layer_idx * n_pool

    qpos = jnp.full((1,), past, jnp.int32)
    bias_p = _rel_bias(rel_table, qpos, jnp.arange(past))[:, 0, :]
    bias_p = jnp.repeat(bias_p, 2, axis=0)
    bias_n = jnp.repeat(_rel_bias(rel_table, qpos, qpos)[:, 0, :], 2, axis=0)

    def page_spec(gi, cols):
        return pl.BlockSpec((1, PAGE_SIZE, cols), lambda b, s, pt: (base + pt[b, s * n_pg + gi], 0, 0))

    row3 = lambda b, s, pt: (b, 0, 0)
    const2 = lambda b, s, pt: (0, 0)
    in_specs = [
        pl.BlockSpec(lam_vecs.shape, const2),
        pl.BlockSpec((1, 1, width), row3),
        pl.BlockSpec((1, 1, width), row3),
        pl.BlockSpec((1, 1, DA_V), row3),
        pl.BlockSpec((2 * DA_HEADS, n_pg * PAGE_SIZE), lambda b, s, pt: (0, s)),
        pl.BlockSpec((2 * DA_HEADS, 1), const2),
        pl.BlockSpec((1, DA_VD), const2),
    ]
    in_specs += [page_spec(gi, width) for gi in range(n_pg)]
    in_specs += [page_spec(gi, DA_V) for gi in range(n_pg)]
    out = pl.pallas_call(
        functools.partial(_attn_sample_kernel, n_pg=n_pg, lam_init=lam_init),
        grid_spec=pltpu.PrefetchScalarGridSpec(
            num_scalar_prefetch=1,
            grid=(bd, n_steps),
            in_specs=in_specs,
            out_specs=pl.BlockSpec((1, 1, DA_V), row3),
            scratch_shapes=[
                pltpu.VMEM((2 * DA_HEADS, 1), F32),
                pltpu.VMEM((2 * DA_HEADS, 1), F32),
                pltpu.VMEM((2 * DA_HEADS, DA_V), F32),
            ],
        ),
        out_shape=jax.ShapeDtypeStruct((bd, 1, DA_V), _MXU_DTYPE),
        compiler_params=_params("parallel", "arbitrary"),
        name="attn_sample",
    )(page_table, lam_vecs, q.reshape(bd, 1, width), k_new.reshape(bd, 1, width), v_new.reshape(bd, 1, DA_V),
      bias_p, bias_n, g_da.reshape(1, DA_VD), *([ck] * n_pg), *([cv] * n_pg))
    return out.reshape(bd, DA_V)


def _gla_prompt_kernel(q_ref, k_ref, v_ref, la_ref, og_ref, g_ref, o_ref, s_ref, st_s, *, chunk):
    t_len = q_ref.shape[1]
    c_sz = chunk
    pair_w = 2 * GLA_DK
    st_s[...] = jnp.zeros(st_s.shape, F32)
    r_i = lax.broadcasted_iota(jnp.int32, (c_sz, c_sz), 0)
    c_i = lax.broadcasted_iota(jnp.int32, (c_sz, c_sz), 1)
    tril = r_i >= c_i
    tril_f = tril.astype(F32)
    tril2 = jnp.concatenate([tril, tril], axis=0)
    lane = lax.broadcasted_iota(jnp.int32, (c_sz, pair_w), 1)
    first = lane < GLA_DK
    g = g_ref[...]

    def split_heads(x):
        zero = jnp.zeros_like(x)
        return jnp.concatenate([jnp.where(first, x, zero), jnp.where(first, zero, x)], axis=0)

    def body(c, carry):
        rows = pl.ds(pl.multiple_of(c * c_sz, c_sz), c_sz)
        la = la_ref[0, rows, :]
        bc = jnp.dot(tril_f, la, precision=lax.Precision.HIGHEST, preferred_element_type=F32)
        b_last = bc[c_sz - 1:c_sz, :]
        q = q_ref[0, rows, :]
        k = k_ref[0, rows, :]
        qe = q * jnp.exp(bc)
        ke = k * jnp.exp(-bc)
        kd = k * jnp.exp(b_last - bc)
        dec = jnp.exp(b_last)
        v = v_ref[0, rows, :]
        og = og_ref[0, rows, :]
        for p in range(GLA_HEADS // 2):
            ps = slice(p * pair_w, (p + 1) * pair_w)
            qs = _mx(split_heads(qe[:, ps]))
            att = jnp.where(tril2, _dot_nt(qs, ke[:, ps]), 0.0)
            st = st_s[p]
            inter = _dot_nt(qs, st)
            va = v[:, (2 * p) * GLA_DV:(2 * p + 1) * GLA_DV]
            vb = v[:, (2 * p + 1) * GLA_DV:(2 * p + 2) * GLA_DV]
            o_a = _dot(att[0:c_sz], va) + inter[0:c_sz]
            o_b = _dot(att[c_sz:2 * c_sz], vb) + inter[c_sz:2 * c_sz]
            upd = _dot_tn(jnp.concatenate([va, vb], axis=0), split_heads(kd[:, ps]))
            st_s[p] = dec[:, ps] * st + upd
            for hh, o_h in ((2 * p, o_a), (2 * p + 1, o_b)):
                cols = slice(hh * GLA_DV, (hh + 1) * GLA_DV)
                o_ref[0, rows, cols] = (_rms(o_h, g) * og[:, cols]).astype(o_ref.dtype)
        return carry

    lax.fori_loop(0, t_len // c_sz, body, 0)
    for p in range(GLA_HEADS // 2):
        s_t = st_s[p].T
        s_ref[0, 2 * p] = s_t[0:GLA_DK]
        s_ref[0, 2 * p + 1] = s_t[GLA_DK:2 * GLA_DK]


def _gla_prompt(gq, gk, gv, la, og, g_gla):
    b, t_len, _ = gq.shape
    chunk = GLA_CHUNK if t_len % GLA_CHUNK == 0 else t_len
    seq = lambda n: pl.BlockSpec((1, t_len, n), lambda bi: (bi, 0, 0))
    return pl.pallas_call(
        functools.partial(_gla_prompt_kernel, chunk=chunk),
        grid=(b,),
        in_specs=[seq(GLA_QK), seq(GLA_QK), seq(GLA_V), seq(GLA_QK), seq(GLA_V), _const_spec((1, GLA_DV))],
        out_specs=[seq(GLA_V), pl.BlockSpec((1, GLA_HEADS, GLA_DK, GLA_DV), lambda bi: (bi, 0, 0, 0))],
        out_shape=[jax.ShapeDtypeStruct((b, t_len, GLA_V), _MXU_DTYPE),
                   jax.ShapeDtypeStruct((b, GLA_HEADS, GLA_DK, GLA_DV), F32)],
        scratch_shapes=[pltpu.VMEM((GLA_HEADS // 2, GLA_DV, 2 * GLA_DK), F32)],
        compiler_params=_params("parallel"),
        name="gla_prompt",
    )(gq, gk, gv, la, og, g_gla.reshape(1, GLA_DV))


def _gla_sample_kernel(qc_ref, kc_ref, lac_ref, v_ref, og_ref, s0_ref, g_ref, o_ref, s_ref):
    g = g_ref[...]
    for h in range(GLA_HEADS):
        bc = lac_ref[0, h]
        qe = qc_ref[0, h] * jnp.exp(bc)
        ke = kc_ref[0, h] * jnp.exp(-bc)
        kd = kc_ref[0, h] * jnp.exp(bc - bc)
        v = v_ref[0, h].astype(F32)
        s0 = s0_ref[0, h]
        att = jnp.sum(qe * ke, axis=0, keepdims=True)
        o = att * v + jnp.sum(qe * s0, axis=0, keepdims=True)
        s_ref[0, h] = jnp.exp(bc) * s0 + kd * v
        o_ref[0, h] = (_rms(o, g) * og_ref[0, h]).astype(o_ref.dtype)


def _gla_sample(gq, gk, gv, la, og, s0, g_gla):
    bd = gq.shape[0]
    col = lambda x: x.reshape(bd, GLA_HEADS, GLA_DK, 1)
    row = lambda x: x.reshape(bd, GLA_HEADS, 1, GLA_DV)
    col_spec = pl.BlockSpec((1, GLA_HEADS, GLA_DK, 1), lambda b: (b, 0, 0, 0))
    row_spec = pl.BlockSpec((1, GLA_HEADS, 1, GLA_DV), lambda b: (b, 0, 0, 0))
    st_spec = pl.BlockSpec((1, GLA_HEADS, GLA_DK, GLA_DV), lambda b: (b, 0, 0, 0))
    o, s_new = pl.pallas_call(
        _gla_sample_kernel,
        grid=(bd,),
        in_specs=[col_spec, col_spec, col_spec, row_spec, row_spec, st_spec, _const_spec((1, GLA_DV))],
        out_specs=[row_spec, st_spec],
        out_shape=[jax.ShapeDtypeStruct((bd, GLA_HEADS, 1, GLA_DV), _MXU_DTYPE),
                   jax.ShapeDtypeStruct((bd, GLA_HEADS, GLA_DK, GLA_DV), F32)],
        compiler_params=_params("parallel"),
        name="gla_sample",
    )(col(gq), col(gk), col(la), row(gv), row(og), s0, g_gla.reshape(1, GLA_DV))
    return o.reshape(bd, GLA_V), s_new


def _rglru_coeffs(xc, wga_ref, bga, wgx_ref, bgx, lam):
    bw = wga_ref.shape[1]
    r_parts, i_parts = [], []
    for n in range(wga_ref.shape[0]):
        xb = _mx(xc[:, n * bw:(n + 1) * bw])
        r_parts.append(_dot(xb, wga_ref[n]))
        i_parts.append(_dot(xb, wgx_ref[n]))
    r = jax.nn.sigmoid(jnp.concatenate(r_parts, axis=1) + bga)
    i = jax.nn.sigmoid(jnp.concatenate(i_parts, axis=1) + bgx)
    log_a = (-RG_C * _softplus(-lam)) * r
    a = jnp.exp(log_a)
    b = jnp.sqrt(-jnp.tanh(log_a) * (a * a + 1.0)) * (i * xc)
    return a, b


def _rglru_prompt_kernel(x_ref, gate_ref, cw_ref, cb_ref, wga_ref, bga_ref, wgx_ref, bgx_ref, lam_ref,
                         y_ref, hl_ref, tail_s, h_s, a_s, b_s):
    tc = x_ref.shape[1]
    pad = tail_s.shape[0]

    @pl.when(pl.program_id(1) == 0)
    def _():
        tail_s[...] = jnp.zeros(tail_s.shape, F32)
        h_s[...] = jnp.zeros(h_s.shape, F32)

    x = x_ref[0]
    xp = jnp.concatenate([tail_s[...], x], axis=0)
    cw = cw_ref[...]
    xc = cb_ref[...]
    for j in range(CONV_W):
        off = pad - (CONV_W - 1) + j
        xc = xc + cw[j:j + 1] * xp[off:off + tc]
    tail_s[...] = x[tc - pad:tc]
    a, b = _rglru_coeffs(xc, wga_ref, bga_ref[...], wgx_ref, bgx_ref[...], lam_ref[...])
    a_s[...] = a
    b_s[...] = b

    def body(t, h):
        row = pl.ds(t, 1)
        h = a_s[row, :] * h + b_s[row, :]
        b_s[row, :] = h
        return h

    h = lax.fori_loop(0, tc, body, h_s[...], unroll=8)
    h_s[...] = h
    hl_ref[0] = h
    y_ref[0] = (gate_ref[0] * b_s[...]).astype(y_ref.dtype)


def _rglru_prompt(x_br, gate, conv_w, conv_b, w_ga, b_ga, w_gx, b_gx, lam):
    b, t_len, d = x_br.shape
    tc = RNN_TIME_TILE if t_len % RNN_TIME_TILE == 0 else t_len
    vec = lambda x: x.reshape(1, d)
    seq = pl.BlockSpec((1, tc, d), lambda bi, ti: (bi, ti, 0))
    y, h_last = pl.pallas_call(
        _rglru_prompt_kernel,
        grid=(b, t_len // tc),
        in_specs=[seq, seq, _const_spec(conv_w.shape), _const_spec((1, d)),
                  _const_spec(w_ga.shape), _const_spec((1, d)), _const_spec(w_gx.shape), _const_spec((1, d)),
                  _const_spec((1, d))],
        out_specs=[seq, pl.BlockSpec((1, 1, d), lambda bi, ti: (bi, 0, 0))],
        out_shape=[jax.ShapeDtypeStruct((b, t_len, d), _MXU_DTYPE), jax.ShapeDtypeStruct((b, 1, d), F32)],
        scratch_shapes=[pltpu.VMEM((V7X_SUBLANES, d), F32), pltpu.VMEM((1, d), F32),
                        pltpu.VMEM((tc, d), F32), pltpu.VMEM((tc, d), F32)],
        compiler_params=_params("parallel", "arbitrary"),
        name="rglru_prompt",
    )(x_br, gate, conv_w, vec(conv_b), w_ga, vec(b_ga), w_gx, vec(b_gx), vec(lam))
    return y, h_last.reshape(b, d)


def _rglru_sample_kernel(x_ref, gate_ref, c0_ref, c1_ref, c2_ref, h0_ref, cw_ref, cb_ref,
                         wga_ref, bga_ref, wgx_ref, bgx_ref, lam_ref, y_ref, h_ref):
    cw = cw_ref[...]
    xc = cb_ref[...] + cw[0:1] * c0_ref[...] + cw[1:2] * c1_ref[...] + cw[2:3] * c2_ref[...] + cw[3:4] * x_ref[...]
    a, b = _rglru_coeffs(xc, wga_ref, bga_ref[...], wgx_ref, bgx_ref[...], lam_ref[...])
    h = a * h0_ref[...] + b
    h_ref[...] = h
    y_ref[...] = (gate_ref[...] * h).astype(y_ref.dtype)


def _rglru_sample(x_br, gate, conv_state, h0, conv_w, conv_b, w_ga, b_ga, w_gx, b_gx, lam):
    bd, d = x_br.shape
    vec = lambda x: x.reshape(1, d)
    args = (x_br, gate, conv_state[:, 0], conv_state[:, 1], conv_state[:, 2], h0, conv_w, vec(conv_b),
            w_ga, vec(b_ga), w_gx, vec(b_gx), vec(lam))
    return pl.pallas_call(
        _rglru_sample_kernel,
        grid=(1,),
        in_specs=[_const_spec(a.shape) for a in args],
        out_specs=[pl.BlockSpec((bd, d), lambda i: (0, 0))] * 2,
        out_shape=[jax.ShapeDtypeStruct((bd, d), _MXU_DTYPE), jax.ShapeDtypeStruct((bd, d), F32)],
        compiler_params=_params("arbitrary"),
        name="rglru_sample",
    )(*args)


def kernel(x_prompt, x_sample, cache_k, cache_v, state_gla, state_conv, state_rglru, page_table, rel_table, norm_g, w_up, w_down, w_in_even, w_alpha2, b_alpha2, lam_q1, lam_k1, lam_q2, lam_k2, g_diff, g_gla, w_out_even, w_in_odd, conv_w, conv_b, w_gate_a, b_gate_a, w_gate_x, b_gate_x, rg_lambda, w_out_odd):
    bp, t_len, d = x_prompt.shape
    bd, dec_seq, _ = x_sample.shape
    assert dec_seq == 1, "the sample group decodes one token per sequence"
    depth = norm_g.shape[0]
    yp = x_prompt.reshape(bp * t_len, d)
    ys = x_sample.reshape(bd, d)
    offs = [int(o) for o in np.cumsum(SPLIT_SIZES)]
    outs = {k: [] for k in ("kp", "vp", "ks", "vs", "gp", "gs", "cp", "cs", "rp", "rs")}

    for layer in range(depth):
        g4 = norm_g[layer]
        li = layer // 2
        wu = _mx(w_up[layer])
        wd = _mx(w_down[layer])
        if layer % 2 == 0:
            lam_init = 0.8 - 0.6 * math.exp(-0.3 * layer)
            w_in = _mx(w_in_even[li])
            w_parts = [w_in[:, a:b] for a, b in zip([0] + offs[:-1], offs)]
            w_a2 = _mx(w_alpha2[li])
            lam_vecs = jnp.stack([lam_q1[li], lam_k1[li], lam_q2[li], lam_k2[li]]).astype(F32)
            w_out = _mx(w_out_even[li])
            wo_parts = [w_out[:DA_V], w_out[DA_V:]]

            q, k, v, gq, gk, gv, la, og = _inproj_even(yp, g4[0], w_parts, w_a2, b_alpha2[li])
            t = min(ATTN_TILE, t_len)
            bias = _prompt_bias(rel_table, t, t_len)
            sh = lambda x: x.reshape(bp, t_len, x.shape[-1])
            a_out = _attn_prompt(sh(q), sh(k), sh(v), lam_vecs, bias, g_diff[li], lam_init)
            o_gla, s_p = _gla_prompt(sh(gq), sh(gk), sh(gv), sh(la), sh(og), g_gla[li])
            yp = _post([a_out.reshape(bp * t_len, DA_V), o_gla.reshape(bp * t_len, GLA_V)], wo_parts, yp, g4, wu, wd)
            outs["kp"].append(k.reshape(bp, t_len, DA_HEADS, 2 * DA_DH))
            outs["vp"].append(v.reshape(bp, t_len, DA_HEADS, DA_VD))
            outs["gp"].append(s_p)

            q, k, v, gq, gk, gv, la, og = _inproj_even(ys, g4[0], w_parts, w_a2, b_alpha2[li])
            a_out = _attn_sample(q, k, v, cache_k, cache_v, li, page_table, lam_vecs, rel_table, g_diff[li], lam_init)
            o_gla, s_s = _gla_sample(gq, gk, gv, la, og, state_gla[li], g_gla[li])
            ys = _post([a_out, o_gla], wo_parts, ys, g4, wu, wd)
            outs["ks"].append(k.reshape(bd, 1, DA_HEADS, 2 * DA_DH))
            outs["vs"].append(v.reshape(bd, 1, DA_HEADS, DA_VD))
            outs["gs"].append(s_s)
        else:
            w_in = _mx(w_in_odd[li])
            d_rnn = w_in.shape[1] // 2
            w_gate, w_x = w_in[:, :d_rnn], w_in[:, d_rnn:]
            wga, wgx = _mx(w_gate_a[li]), _mx(w_gate_x[li])
            w_out = [_mx(w_out_odd[li])]
            rnn = (conv_w[li], conv_b[li], wga, b_gate_a[li], wgx, b_gate_x[li], rg_lambda[li])

            gate, x_br = _inproj_odd(yp, g4[0], w_gate, w_x)
            x_br3 = x_br.reshape(bp, t_len, d_rnn)
            y, h_last = _rglru_prompt(x_br3, gate.reshape(bp, t_len, d_rnn), *rnn)
            yp = _post([y.reshape(bp * t_len, d_rnn)], w_out, yp, g4, wu, wd)
            if t_len >= CONV_W - 1:
                new_buf = x_br3[:, t_len - (CONV_W - 1):]
            else:
                new_buf = jnp.concatenate([jnp.zeros((bp, CONV_W - 1 - t_len, d_rnn), F32), x_br3], axis=1)
            outs["cp"].append(new_buf)
            outs["rp"].append(h_last)

            gate, x_br = _inproj_odd(ys, g4[0], w_gate, w_x)
            y, h_new = _rglru_sample(x_br, gate, state_conv[li], state_rglru[li], *rnn)
            ys = _post([y], w_out, ys, g4, wu, wd)
            outs["cs"].append(jnp.concatenate([state_conv[li][:, 1:], x_br[:, None, :]], axis=1))
            outs["rs"].append(h_new)

    st = lambda key: jnp.stack(outs[key])
    return (yp.reshape(bp, t_len, d), ys.reshape(bd, 1, d), st("kp"), st("vp"), st("ks"), st("vs"),
            st("gp"), st("gs"), st("cp"), st("cs"), st("rp"), st("rs"))
```

```python
import functools
import math

import numpy as np
import jax
import jax.numpy as jnp
from jax import lax
from jax.experimental import pallas as pl
from jax.experimental.pallas import tpu as pltpu

DA_HEADS = 4
DA_DH = 64
DA_VD = 2 * DA_DH
GLA_HEADS = 4
GLA_DK = 64
GLA_DV = 128
GLA_RANK = 16
GLA_TAU = 16.0
DA_QK = DA_HEADS * 2 * DA_DH
DA_V = DA_HEADS * DA_VD
GLA_QK = GLA_HEADS * GLA_DK
GLA_V = GLA_HEADS * GLA_DV
SPLIT_SIZES = (DA_QK, DA_QK, DA_V, GLA_QK, GLA_QK, GLA_V, GLA_RANK, GLA_V)
RNN_BLOCKS = 8
CONV_W = 4
RG_C = 8.0
NUM_BUCKETS = 32
MAX_DISTANCE = 128
PAGE_SIZE = 128
EPS = 1e-6
NEG = -1e30

V7X_LANES = 128
V7X_SUBLANES = 8
V7X_VMEM_LIMIT_BYTES = 56 * 1024 * 1024

F32 = jnp.float32
_MXU_DTYPE = jnp.bfloat16

ROW_TILE = 512
FF_CHUNK = 1024
ATTN_TILE = 256
ATTN_HEADS_PER_STEP = 4
GLA_CHUNK = 64
RNN_TIME_TILE = 256
PAGES_PER_STEP = 8


def _mx(x):
    return x.astype(_MXU_DTYPE)


def _dot(a, b):
    return jnp.dot(_mx(a), _mx(b), preferred_element_type=F32)


def _dot_nt(a, b):
    return lax.dot_general(_mx(a), _mx(b), (((1,), (1,)), ((), ())), preferred_element_type=F32)


def _dot_tn(a, b):
    return lax.dot_general(_mx(a), _mx(b), (((0,), (0,)), ((), ())), preferred_element_type=F32)


def _rms(x, g):
    return x * lax.rsqrt(jnp.mean(x * x, axis=-1, keepdims=True) + EPS) * g


def _softplus(x):
    return jnp.maximum(x, 0.0) + jnp.log1p(jnp.exp(-jnp.abs(x)))


def _log_sigmoid(x):
    return -_softplus(-x)


def _gelu_tanh(x):
    c = math.sqrt(2.0 / math.pi)
    return 0.5 * x * (1.0 + jnp.tanh(c * (x + 0.044715 * (x * x * x))))


def _params(*sem):
    return pltpu.CompilerParams(dimension_semantics=sem, vmem_limit_bytes=V7X_VMEM_LIMIT_BYTES)


def _const_spec(shape):
    nd = len(shape)
    return pl.BlockSpec(shape, lambda *_: (0,) * nd, pipeline_mode=pl.Buffered(1))


def _row_tile(m):
    return ROW_TILE if m % ROW_TILE == 0 else m


def _inproj_even_kernel(x_ref, g_ref, wq, wk, wv, wgq, wgk, wgv, wlr, wog, wa2, ba2,
                        q_o, k_o, v_o, gq_o, gk_o, gv_o, la_o, og_o):
    h = _mx(_rms(x_ref[...], g_ref[...]))
    q_o[...] = (_dot(h, wq[...]) * (DA_DH ** -0.5)).astype(q_o.dtype)
    k_o[...] = _dot(h, wk[...])
    v_o[...] = _dot(h, wv[...])
    gq_o[...] = _dot(h, wgq[...]) * (GLA_DK ** -0.5)
    gk_o[...] = _dot(h, wgk[...])
    gv_o[...] = _dot(h, wgv[...]).astype(gv_o.dtype)
    z = _dot(_dot(h, wlr[...]), wa2[...]) + ba2[...]
    la_o[...] = _log_sigmoid(z) * (1.0 / GLA_TAU)
    og = _dot(h, wog[...])
    og_o[...] = og * jax.nn.sigmoid(og)


def _inproj_even(x, g, w_parts, w_a2, b_a2):
    m, d = x.shape
    tm = _row_tile(m)
    out_cols = (DA_QK, DA_QK, DA_V, GLA_QK, GLA_QK, GLA_V, GLA_QK, GLA_V)
    out_dtypes = (_MXU_DTYPE, F32, F32, F32, F32, _MXU_DTYPE, F32, F32)
    in_specs = [pl.BlockSpec((tm, d), lambda i: (i, 0)), _const_spec((1, d))]
    in_specs += [_const_spec(w.shape) for w in w_parts]
    in_specs += [_const_spec(w_a2.shape), _const_spec((1, GLA_QK))]
    return pl.pallas_call(
        _inproj_even_kernel,
        grid=(m // tm,),
        in_specs=in_specs,
        out_specs=[pl.BlockSpec((tm, n), lambda i: (i, 0)) for n in out_cols],
        out_shape=[jax.ShapeDtypeStruct((m, n), dt) for n, dt in zip(out_cols, out_dtypes)],
        compiler_params=_params("parallel"),
        name="inproj_even",
    )(x, g.reshape(1, d), *w_parts, w_a2, b_a2.reshape(1, GLA_QK))


def _inproj_odd_kernel(x_ref, g_ref, wg, wx, gate_o, x_o):
    h = _mx(_rms(x_ref[...], g_ref[...]))
    gate_o[...] = _gelu_tanh(_dot(h, wg[...]))
    x_o[...] = _dot(h, wx[...])


def _inproj_odd(x, g, w_gate, w_x):
    m, d = x.shape
    tm = _row_tile(m)
    n = w_gate.shape[1]
    return pl.pallas_call(
        _inproj_odd_kernel,
        grid=(m // tm,),
        in_specs=[pl.BlockSpec((tm, d), lambda i: (i, 0)), _const_spec((1, d)),
                  _const_spec(w_gate.shape), _const_spec(w_x.shape)],
        out_specs=[pl.BlockSpec((tm, n), lambda i: (i, 0))] * 2,
        out_shape=[jax.ShapeDtypeStruct((m, n), F32)] * 2,
        compiler_params=_params("parallel"),
        name="inproj_odd",
    )(x, g.reshape(1, d), w_gate, w_x)


def _post_kernel(*refs, n_mix):
    mix_refs = refs[:n_mix]
    wo_refs = refs[n_mix:2 * n_mix]
    x_ref, g_ref, wup_ref, wdn_ref, y_ref = refs[2 * n_mix:]
    m = _dot(mix_refs[0][...], wo_refs[0][...])
    for mr, wr in zip(mix_refs[1:], wo_refs[1:]):
        m = m + _dot(mr[...], wr[...])
    g = g_ref[...]
    y1 = x_ref[...] + _rms(m, g[1:2])
    h = _mx(_rms(y1, g[2:3]))
    d_ff = wup_ref.shape[1]
    fc = min(FF_CHUNK, d_ff)
    acc = None
    for c in range(d_ff // fc):
        a = jnp.square(jnp.maximum(_dot(h, wup_ref[:, c * fc:(c + 1) * fc]), 0.0))
        part = _dot(a, wdn_ref[c * fc:(c + 1) * fc, :])
        acc = part if acc is None else acc + part
    y_ref[...] = y1 + _rms(acc, g[3:4])


def _post(mix_parts, wo_parts, x, g4, w_up, w_down):
    m, d = x.shape
    tm = _row_tile(m)
    n_mix = len(mix_parts)
    in_specs = [pl.BlockSpec((tm, p.shape[1]), lambda i: (i, 0)) for p in mix_parts]
    in_specs += [_const_spec(w.shape) for w in wo_parts]
    in_specs += [pl.BlockSpec((tm, d), lambda i: (i, 0)), _const_spec(g4.shape),
                 _const_spec(w_up.shape), _const_spec(w_down.shape)]
    return pl.pallas_call(
        functools.partial(_post_kernel, n_mix=n_mix),
        grid=(m // tm,),
        in_specs=in_specs,
        out_specs=pl.BlockSpec((tm, d), lambda i: (i, 0)),
        out_shape=jax.ShapeDtypeStruct((m, d), F32),
        compiler_params=_params("parallel"),
        name="post_mlp",
    )(*mix_parts, *wo_parts, x, g4, w_up, w_down)


def _t5_bucket(rel):
    rel = jnp.maximum(rel, 0)
    max_exact = NUM_BUCKETS // 2
    large = max_exact + (jnp.log(jnp.maximum(rel, 1).astype(F32) / max_exact)
                         / math.log(MAX_DISTANCE / max_exact) * (NUM_BUCKETS - max_exact)).astype(jnp.int32)
    large = jnp.minimum(large, NUM_BUCKETS - 1)
    return jnp.where(rel < max_exact, rel, large)


def _rel_bias(table, qpos, kpos):
    bucket = _t5_bucket(qpos[:, None] - kpos[None, :])
    tab = table.astype(F32)
    out = jnp.zeros((tab.shape[1],) + bucket.shape, F32)
    for bk in range(NUM_BUCKETS):
        out = jnp.where((bucket == bk)[None], tab[bk][:, None, None], out)
    return out


def _check_far_bucket_constant(first, last):
    rel = np.arange(first, last + 1, dtype=np.float64)
    max_exact = NUM_BUCKETS // 2
    steps = np.log(rel / max_exact) / math.log(MAX_DISTANCE / max_exact) * (NUM_BUCKETS - max_exact)
    assert first >= max_exact and np.all(steps >= NUM_BUCKETS - max_exact - 0.5), "far keys must share one bias bucket"


def _lambda_from(lam_ref, lam_init):
    lv = lam_ref[...]
    s1 = jnp.sum(lv[0:1] * lv[1:2], axis=-1, keepdims=True)
    s2 = jnp.sum(lv[2:3] * lv[3:4], axis=-1, keepdims=True)
    return jnp.exp(s1) - jnp.exp(s2) + lam_init


def _attn_prompt_kernel(lam_ref, q_ref, k_ref, v_ref, bias_ref, g_ref, o_ref,
                        kb, vt, m_s, l_s, acc_s, *, t, nh, lam_init):
    i = pl.program_id(2)
    n_blk = vt.shape[0]
    hd = 2 * DA_DH

    @pl.when(i == 0)
    def _():
        kb[...] = _mx(k_ref[0])
        for jj in range(n_blk):
            for h in range(nh):
                vt[jj, h] = _mx(v_ref[0, jj * t:(jj + 1) * t, h * DA_VD:(h + 1) * DA_VD].T)

    sub = lax.broadcasted_iota(jnp.int32, (hd, t), 0)
    zero = jnp.zeros((hd, t), F32)
    qs_t = []
    for h in range(nh):
        q_t = q_ref[0, :, h * hd:(h + 1) * hd].astype(F32).T
        qs_t.append(_mx(jnp.concatenate([jnp.where(sub < DA_DH, q_t, zero), jnp.where(sub >= DA_DH, q_t, zero)],
                                        axis=1)))

    m_s[...] = jnp.full(m_s.shape, -jnp.inf, F32)
    l_s[...] = jnp.zeros(l_s.shape, F32)
    acc_s[...] = jnp.zeros(acc_s.shape, F32)

    def step(j, tile):
        rows = pl.ds(pl.multiple_of(j * t, t), t)
        for h in range(nh):
            s = _dot(kb[rows, h * hd:(h + 1) * hd], qs_t[h])
            if tile is not None:
                s = s + bias_ref[h, tile]
            m_old = m_s[h]
            m_new = jnp.maximum(m_old, jnp.max(s, axis=0, keepdims=True))
            alpha = jnp.exp(m_old - m_new)
            p = jnp.exp(s - m_new)
            l_s[h] = alpha * l_s[h] + jnp.sum(p, axis=0, keepdims=True)
            acc_s[h] = alpha * acc_s[h] + _dot(vt[j, h], p)
            m_s[h] = m_new

    def far(j, carry):
        step(j, None)
        return carry

    lax.fori_loop(0, jnp.maximum(i - 1, 0), far, 0)

    @pl.when(i > 0)
    def _():
        step(i - 1, 0)

    step(i, 1)

    lam = _lambda_from(lam_ref, lam_init)
    for h in range(nh):
        an = acc_s[h] / l_s[h]
        o = (an[:, 0:t] - lam * an[:, t:2 * t]).T
        o_ref[0, :, h * DA_VD:(h + 1) * DA_VD] = (_rms(o, g_ref[...]) * (1.0 - lam_init)).astype(o_ref.dtype)


def _attn_prompt(q, k, v, lam_vecs, bias, g_da, lam_init):
    b, t_len, _ = q.shape
    t = min(ATTN_TILE, t_len)
    nq = t_len // t
    nh = ATTN_HEADS_PER_STEP
    hd = 2 * DA_DH
    return pl.pallas_call(
        functools.partial(_attn_prompt_kernel, t=t, nh=nh, lam_init=lam_init),
        grid=(b, DA_HEADS // nh, nq),
        in_specs=[
            _const_spec(lam_vecs.shape),
            pl.BlockSpec((1, t, nh * hd), lambda bi, h, i: (bi, i, h)),
            pl.BlockSpec((1, t_len, nh * hd), lambda bi, h, i: (bi, 0, h)),
            pl.BlockSpec((1, t_len, nh * DA_VD), lambda bi, h, i: (bi, 0, h)),
            pl.BlockSpec((nh, 2, t, 2 * t), lambda bi, h, i: (h, 0, 0, 0)),
            _const_spec((1, DA_VD)),
        ],
        out_specs=pl.BlockSpec((1, t, nh * DA_VD), lambda bi, h, i: (bi, i, h)),
        out_shape=jax.ShapeDtypeStruct((b, t_len, DA_V), _MXU_DTYPE),
        scratch_shapes=[
            pltpu.VMEM((t_len, nh * hd), _MXU_DTYPE),
            pltpu.VMEM((nq, nh, DA_VD, t), _MXU_DTYPE),
            pltpu.VMEM((nh, 1, 2 * t), F32),
            pltpu.VMEM((nh, 1, 2 * t), F32),
            pltpu.VMEM((nh, DA_VD, 2 * t), F32),
        ],
        compiler_params=_params("parallel", "parallel", "arbitrary"),
        name="attn_prompt",
    )(lam_vecs, q, k, v, bias, g_da.reshape(1, DA_VD))


def _prompt_bias(rel_table, t, t_len):
    qpos = t + jnp.arange(t)
    kpos = jnp.arange(2 * t)
    bias = _rel_bias(rel_table, qpos, kpos)
    if t_len > 2 * t:
        _check_far_bucket_constant(t + 1, t_len)
    far = rel_table[NUM_BUCKETS - 1].astype(F32)
    bias = bias - far[:, None, None]
    mask = qpos[:, None] >= kpos[None, :]
    bias = jnp.where(mask[None], bias, NEG)
    bias = jnp.swapaxes(bias, 1, 2).reshape(bias.shape[0], 2, t, t)
    return jnp.concatenate([bias, bias], axis=-1)


def _attn_sample_kernel(pt_ref, lam_ref, q_ref, kn_ref, vn_ref, bias_ref, biasn_ref, g_ref, *rest,
                        n_pg, lam_init):
    del pt_ref
    k_refs = rest[:n_pg]
    v_refs = rest[n_pg:2 * n_pg]
    o_ref, m_s, l_s, acc_s = rest[2 * n_pg:]
    step_i = pl.program_id(1)
    n_steps = pl.num_programs(1)
    rows8 = 2 * DA_HEADS
    hd = 2 * DA_DH
    pg_cols = PAGE_SIZE * DA_HEADS

    @pl.when(step_i == 0)
    def _():
        m_s[...] = jnp.full(m_s.shape, -jnp.inf, F32)
        l_s[...] = jnp.zeros(l_s.shape, F32)
        acc_s[...] = jnp.zeros(acc_s.shape, F32)

    def rep2(x):
        return jnp.concatenate([x[r // 2:r // 2 + 1] for r in range(rows8)], axis=0)

    rowi = lax.broadcasted_iota(jnp.int32, (rows8, hd), 0)
    lanei = lax.broadcasted_iota(jnp.int32, (rows8, hd), 1)
    q8 = jnp.where(lanei // DA_DH == rowi % 2, rep2(q_ref[0].astype(F32)), 0.0)

    s = jnp.concatenate([_dot_nt(q8, kr[0]) for kr in k_refs], axis=1) + bias_ref[...]
    m_old = m_s[...]
    m_new = jnp.maximum(m_old, jnp.max(s, axis=-1, keepdims=True))
    alpha = jnp.exp(m_old - m_new)
    p = jnp.exp(s - m_new)
    l_new = alpha * l_s[...] + jnp.sum(p, axis=-1, keepdims=True)
    pv = _dot(p[:, 0:pg_cols], v_refs[0][0])
    for gi in range(1, n_pg):
        pv = pv + _dot(p[:, gi * pg_cols:(gi + 1) * pg_cols], v_refs[gi][0])
    acc_new = alpha * acc_s[...] + pv
    m_s[...] = m_new
    l_s[...] = l_new
    acc_s[...] = acc_new

    @pl.when(step_i == n_steps - 1)
    def _():
        sn = jnp.sum(q8 * rep2(kn_ref[0]), axis=-1, keepdims=True) + biasn_ref[...]
        m_fin = jnp.maximum(m_new, sn)
        a2 = jnp.exp(m_new - m_fin)
        pn = jnp.exp(sn - m_fin)
        l_fin = a2 * l_new + pn
        acc_fin = a2 * acc_new + pn * rep2(vn_ref[0])
        an = acc_fin / l_fin
        lam = _lambda_from(lam_ref, lam_init)
        o = jnp.concatenate([an[2 * h:2 * h + 1] - lam * an[2 * h + 1:2 * h + 2] for h in range(DA_HEADS)], axis=0)
        o_ref[0] = (_rms(o, g_ref[...]) * (1.0 - lam_init)).astype(o_ref.dtype)


def _attn_sample(q, k_new, v_new, cache_k, cache_v, layer_idx, page_table, lam_vecs, rel_table, g_da, lam_init):
    bd = q.shape[0]
    n_pages = page_table.shape[1]
    past = n_pages * PAGE_SIZE
    n_pool = cache_k.shape[1]
    hd = 2 * DA_DH
    pg_rows = PAGE_SIZE * DA_HEADS
    n_pg = PAGES_PER_STEP if n_pages % PAGES_PER_STEP == 0 else 1
    n_steps = n_pages // n_pg
    ck = cache_k.reshape(cache_k.shape[0] * n_pool, pg_rows, hd)
    cv = cache_v.reshape(cache_v.shape[0] * n_pool, pg_rows, DA_VD)
    base = layer_idx * n_pool

    qpos = jnp.full((1,), past, jnp.int32)
    bias_p = _rel_bias(rel_table, qpos, jnp.arange(past))[:, 0, :]
    own_head = jnp.eye(DA_HEADS, dtype=bool)[:, None, :]
    bias_p = jnp.where(own_head, bias_p[:, :, None], NEG).reshape(DA_HEADS, past * DA_HEADS)
    bias_p = jnp.repeat(bias_p, 2, axis=0)
    bias_n = jnp.repeat(_rel_bias(rel_table, qpos, qpos)[:, 0, :], 2, axis=0)

    def page_spec(gi):
        return pl.BlockSpec((1, pg_rows, hd), lambda b, s, pt: (base + pt[b, s * n_pg + gi], 0, 0))

    row3 = lambda b, s, pt: (b, 0, 0)
    const2 = lambda b, s, pt: (0, 0)
    in_specs = [
        pl.BlockSpec(lam_vecs.shape, const2),
        pl.BlockSpec((1, DA_HEADS, hd), row3),
        pl.BlockSpec((1, DA_HEADS, hd), row3),
        pl.BlockSpec((1, DA_HEADS, DA_VD), row3),
        pl.BlockSpec((2 * DA_HEADS, n_pg * pg_rows), lambda b, s, pt: (0, s)),
        pl.BlockSpec((2 * DA_HEADS, 1), const2),
        pl.BlockSpec((1, DA_VD), const2),
    ]
    in_specs += [page_spec(gi) for gi in range(n_pg)] + [page_spec(gi) for gi in range(n_pg)]
    out = pl.pallas_call(
        functools.partial(_attn_sample_kernel, n_pg=n_pg, lam_init=lam_init),
        grid_spec=pltpu.PrefetchScalarGridSpec(
            num_scalar_prefetch=1,
            grid=(bd, n_steps),
            in_specs=in_specs,
            out_specs=pl.BlockSpec((1, DA_HEADS, DA_VD), row3),
            scratch_shapes=[
                pltpu.VMEM((2 * DA_HEADS, 1), F32),
                pltpu.VMEM((2 * DA_HEADS, 1), F32),
                pltpu.VMEM((2 * DA_HEADS, DA_VD), F32),
            ],
        ),
        out_shape=jax.ShapeDtypeStruct((bd, DA_HEADS, DA_VD), _MXU_DTYPE),
        compiler_params=_params("parallel", "arbitrary"),
        name="attn_sample",
    )(page_table, lam_vecs, q.reshape(bd, DA_HEADS, hd), k_new.reshape(bd, DA_HEADS, hd),
      v_new.reshape(bd, DA_HEADS, DA_VD), bias_p, bias_n, g_da.reshape(1, DA_VD), *([ck] * n_pg), *([cv] * n_pg))
    return out.reshape(bd, DA_V)


def _gla_prompt_kernel(q_ref, k_ref, v_ref, la_ref, og_ref, g_ref, o_ref, s_ref, st_s, *, chunk):
    t_len = q_ref.shape[1]
    c_sz = chunk
    pair_w = 2 * GLA_DK
    st_s[...] = jnp.zeros(st_s.shape, F32)
    r_i = lax.broadcasted_iota(jnp.int32, (c_sz, c_sz), 0)
    c_i = lax.broadcasted_iota(jnp.int32, (c_sz, c_sz), 1)
    tril = r_i >= c_i
    tril_f = tril.astype(F32)
    tril2 = jnp.concatenate([tril, tril], axis=0)
    lane = lax.broadcasted_iota(jnp.int32, (c_sz, pair_w), 1)
    first = lane < GLA_DK
    g = g_ref[...]

    def split_heads(x):
        zero = jnp.zeros_like(x)
        return jnp.concatenate([jnp.where(first, x, zero), jnp.where(first, zero, x)], axis=0)

    def body(c, carry):
        rows = pl.ds(pl.multiple_of(c * c_sz, c_sz), c_sz)
        la = la_ref[0, rows, :]
        bc = jnp.dot(tril_f, la, precision=lax.Precision.HIGHEST, preferred_element_type=F32)
        b_last = bc[c_sz - 1:c_sz, :]
        q = q_ref[0, rows, :]
        k = k_ref[0, rows, :]
        qe = q * jnp.exp(bc)
        ke = k * jnp.exp(-bc)
        kd = k * jnp.exp(b_last - bc)
        dec = jnp.exp(b_last)
        v = v_ref[0, rows, :]
        og = og_ref[0, rows, :]
        for p in range(GLA_HEADS // 2):
            ps = slice(p * pair_w, (p + 1) * pair_w)
            qs = _mx(split_heads(qe[:, ps]))
            att = jnp.where(tril2, _dot_nt(qs, ke[:, ps]), 0.0)
            st = st_s[p]
            inter = _dot_nt(qs, st)
            va = v[:, (2 * p) * GLA_DV:(2 * p + 1) * GLA_DV]
            vb = v[:, (2 * p + 1) * GLA_DV:(2 * p + 2) * GLA_DV]
            o_a = _dot(att[0:c_sz], va) + inter[0:c_sz]
            o_b = _dot(att[c_sz:2 * c_sz], vb) + inter[c_sz:2 * c_sz]
            upd = _dot_tn(jnp.concatenate([va, vb], axis=0), split_heads(kd[:, ps]))
            st_s[p] = dec[:, ps] * st + upd
            for hh, o_h in ((2 * p, o_a), (2 * p + 1, o_b)):
                cols = slice(hh * GLA_DV, (hh + 1) * GLA_DV)
                o_ref[0, rows, cols] = (_rms(o_h, g) * og[:, cols]).astype(o_ref.dtype)
        return carry

    lax.fori_loop(0, t_len // c_sz, body, 0)
    for p in range(GLA_HEADS // 2):
        s_t = st_s[p].T
        s_ref[0, 2 * p] = s_t[0:GLA_DK]
        s_ref[0, 2 * p + 1] = s_t[GLA_DK:2 * GLA_DK]


def _gla_prompt(gq, gk, gv, la, og, g_gla):
    b, t_len, _ = gq.shape
    chunk = GLA_CHUNK if t_len % GLA_CHUNK == 0 else t_len
    seq = lambda n: pl.BlockSpec((1, t_len, n), lambda bi: (bi, 0, 0))
    return pl.pallas_call(
        functools.partial(_gla_prompt_kernel, chunk=chunk),
        grid=(b,),
        in_specs=[seq(GLA_QK), seq(GLA_QK), seq(GLA_V), seq(GLA_QK), seq(GLA_V), _const_spec((1, GLA_DV))],
        out_specs=[seq(GLA_V), pl.BlockSpec((1, GLA_HEADS, GLA_DK, GLA_DV), lambda bi: (bi, 0, 0, 0))],
        out_shape=[jax.ShapeDtypeStruct((b, t_len, GLA_V), _MXU_DTYPE),
                   jax.ShapeDtypeStruct((b, GLA_HEADS, GLA_DK, GLA_DV), F32)],
        scratch_shapes=[pltpu.VMEM((GLA_HEADS // 2, GLA_DV, 2 * GLA_DK), F32)],
        compiler_params=_params("parallel"),
        name="gla_prompt",
    )(gq, gk, gv, la, og, g_gla.reshape(1, GLA_DV))


def _gla_sample_kernel(qc_ref, kc_ref, lac_ref, v_ref, og_ref, s0_ref, g_ref, o_ref, s_ref):
    g = g_ref[...]
    for h in range(GLA_HEADS):
        bc = lac_ref[0, h]
        qe = qc_ref[0, h] * jnp.exp(bc)
        ke = kc_ref[0, h] * jnp.exp(-bc)
        kd = kc_ref[0, h] * jnp.exp(bc - bc)
        v = v_ref[0, h].astype(F32)
        s0 = s0_ref[0, h]
        att = jnp.sum(qe * ke, axis=0, keepdims=True)
        o = att * v + jnp.sum(qe * s0, axis=0, keepdims=True)
        s_ref[0, h] = jnp.exp(bc) * s0 + kd * v
        o_ref[0, h] = (_rms(o, g) * og_ref[0, h]).astype(o_ref.dtype)


def _gla_sample(gq, gk, gv, la, og, s0, g_gla):
    bd = gq.shape[0]
    col = lambda x: x.reshape(bd, GLA_HEADS, GLA_DK, 1)
    row = lambda x: x.reshape(bd, GLA_HEADS, 1, GLA_DV)
    col_spec = pl.BlockSpec((1, GLA_HEADS, GLA_DK, 1), lambda b: (b, 0, 0, 0))
    row_spec = pl.BlockSpec((1, GLA_HEADS, 1, GLA_DV), lambda b: (b, 0, 0, 0))
    st_spec = pl.BlockSpec((1, GLA_HEADS, GLA_DK, GLA_DV), lambda b: (b, 0, 0, 0))
    o, s_new = pl.pallas_call(
        _gla_sample_kernel,
        grid=(bd,),
        in_specs=[col_spec, col_spec, col_spec, row_spec, row_spec, st_spec, _const_spec((1, GLA_DV))],
        out_specs=[row_spec, st_spec],
        out_shape=[jax.ShapeDtypeStruct((bd, GLA_HEADS, 1, GLA_DV), _MXU_DTYPE),
                   jax.ShapeDtypeStruct((bd, GLA_HEADS, GLA_DK, GLA_DV), F32)],
        compiler_params=_params("parallel"),
        name="gla_sample",
    )(col(gq), col(gk), col(la), row(gv), row(og), s0, g_gla.reshape(1, GLA_DV))
    return o.reshape(bd, GLA_V), s_new


def _rglru_coeffs(xc, wga_ref, bga, wgx_ref, bgx, lam):
    bw = wga_ref.shape[1]
    r_parts, i_parts = [], []
    for n in range(wga_ref.shape[0]):
        xb = _mx(xc[:, n * bw:(n + 1) * bw])
        r_parts.append(_dot(xb, wga_ref[n]))
        i_parts.append(_dot(xb, wgx_ref[n]))
    r = jax.nn.sigmoid(jnp.concatenate(r_parts, axis=1) + bga)
    i = jax.nn.sigmoid(jnp.concatenate(i_parts, axis=1) + bgx)
    log_a = (-RG_C * _softplus(-lam)) * r
    a = jnp.exp(log_a)
    b = jnp.sqrt(-jnp.tanh(log_a) * (a * a + 1.0)) * (i * xc)
    return a, b


def _rglru_prompt_kernel(x_ref, gate_ref, cw_ref, cb_ref, wga_ref, bga_ref, wgx_ref, bgx_ref, lam_ref,
                         y_ref, hl_ref, tail_s, h_s, a_s, b_s):
    tc = x_ref.shape[1]
    pad = tail_s.shape[0]

    @pl.when(pl.program_id(1) == 0)
    def _():
        tail_s[...] = jnp.zeros(tail_s.shape, F32)
        h_s[...] = jnp.zeros(h_s.shape, F32)

    x = x_ref[0]
    xp = jnp.concatenate([tail_s[...], x], axis=0)
    cw = cw_ref[...]
    xc = cb_ref[...]
    for j in range(CONV_W):
        off = pad - (CONV_W - 1) + j
        xc = xc + cw[j:j + 1] * xp[off:off + tc]
    tail_s[...] = x[tc - pad:tc]
    a, b = _rglru_coeffs(xc, wga_ref, bga_ref[...], wgx_ref, bgx_ref[...], lam_ref[...])
    a_s[...] = a
    b_s[...] = b

    def body(t, h):
        row = pl.ds(t, 1)
        h = a_s[row, :] * h + b_s[row, :]
        b_s[row, :] = h
        return h

    h = lax.fori_loop(0, tc, body, h_s[...], unroll=8)
    h_s[...] = h
    hl_ref[0] = h
    y_ref[0] = (gate_ref[0] * b_s[...]).astype(y_ref.dtype)


def _rglru_prompt(x_br, gate, conv_w, conv_b, w_ga, b_ga, w_gx, b_gx, lam):
    b, t_len, d = x_br.shape
    tc = RNN_TIME_TILE if t_len % RNN_TIME_TILE == 0 else t_len
    vec = lambda x: x.reshape(1, d)
    seq = pl.BlockSpec((1, tc, d), lambda bi, ti: (bi, ti, 0))
    y, h_last = pl.pallas_call(
        _rglru_prompt_kernel,
        grid=(b, t_len // tc),
        in_specs=[seq, seq, _const_spec(conv_w.shape), _const_spec((1, d)),
                  _const_spec(w_ga.shape), _const_spec((1, d)), _const_spec(w_gx.shape), _const_spec((1, d)),
                  _const_spec((1, d))],
        out_specs=[seq, pl.BlockSpec((1, 1, d), lambda bi, ti: (bi, 0, 0))],
        out_shape=[jax.ShapeDtypeStruct((b, t_len, d), _MXU_DTYPE), jax.ShapeDtypeStruct((b, 1, d), F32)],
        scratch_shapes=[pltpu.VMEM((V7X_SUBLANES, d), F32), pltpu.VMEM((1, d), F32),
                        pltpu.VMEM((tc, d), F32), pltpu.VMEM((tc, d), F32)],
        compiler_params=_params("parallel", "arbitrary"),
        name="rglru_prompt",
    )(x_br, gate, conv_w, vec(conv_b), w_ga, vec(b_ga), w_gx, vec(b_gx), vec(lam))
    return y, h_last.reshape(b, d)


def _rglru_sample_kernel(x_ref, gate_ref, c0_ref, c1_ref, c2_ref, h0_ref, cw_ref, cb_ref,
                         wga_ref, bga_ref, wgx_ref, bgx_ref, lam_ref, y_ref, h_ref):
    cw = cw_ref[...]
    xc = cb_ref[...] + cw[0:1] * c0_ref[...] + cw[1:2] * c1_ref[...] + cw[2:3] * c2_ref[...] + cw[3:4] * x_ref[...]
    a, b = _rglru_coeffs(xc, wga_ref, bga_ref[...], wgx_ref, bgx_ref[...], lam_ref[...])
    h = a * h0_ref[...] + b
    h_ref[...] = h
    y_ref[...] = (gate_ref[...] * h).astype(y_ref.dtype)


def _rglru_sample(x_br, gate, conv_state, h0, conv_w, conv_b, w_ga, b_ga, w_gx, b_gx, lam):
    bd, d = x_br.shape
    vec = lambda x: x.reshape(1, d)
    args = (x_br, gate, conv_state[:, 0], conv_state[:, 1], conv_state[:, 2], h0, conv_w, vec(conv_b),
            w_ga, vec(b_ga), w_gx, vec(b_gx), vec(lam))
    return pl.pallas_call(
        _rglru_sample_kernel,
        grid=(1,),
        in_specs=[_const_spec(a.shape) for a in args],
        out_specs=[pl.BlockSpec((bd, d), lambda i: (0, 0))] * 2,
        out_shape=[jax.ShapeDtypeStruct((bd, d), _MXU_DTYPE), jax.ShapeDtypeStruct((bd, d), F32)],
        compiler_params=_params("arbitrary"),
        name="rglru_sample",
    )(*args)


def kernel(x_prompt, x_sample, cache_k, cache_v, state_gla, state_conv, state_rglru, page_table, rel_table, norm_g, w_up, w_down, w_in_even, w_alpha2, b_alpha2, lam_q1, lam_k1, lam_q2, lam_k2, g_diff, g_gla, w_out_even, w_in_odd, conv_w, conv_b, w_gate_a, b_gate_a, w_gate_x, b_gate_x, rg_lambda, w_out_odd):
    bp, t_len, d = x_prompt.shape
    bd, dec_seq, _ = x_sample.shape
    assert dec_seq == 1, "the sample group decodes one token per sequence"
    depth = norm_g.shape[0]
    yp = x_prompt.reshape(bp * t_len, d)
    ys = x_sample.reshape(bd, d)
    offs = [int(o) for o in np.cumsum(SPLIT_SIZES)]
    outs = {k: [] for k in ("kp", "vp", "ks", "vs", "gp", "gs", "cp", "cs", "rp", "rs")}

    for layer in range(depth):
        g4 = norm_g[layer]
        li = layer // 2
        wu = _mx(w_up[layer])
        wd = _mx(w_down[layer])
        if layer % 2 == 0:
            lam_init = 0.8 - 0.6 * math.exp(-0.3 * layer)
            w_in = _mx(w_in_even[li])
            w_parts = [w_in[:, a:b] for a, b in zip([0] + offs[:-1], offs)]
            w_a2 = _mx(w_alpha2[li])
            lam_vecs = jnp.stack([lam_q1[li], lam_k1[li], lam_q2[li], lam_k2[li]]).astype(F32)
            w_out = _mx(w_out_even[li])
            wo_parts = [w_out[:DA_V], w_out[DA_V:]]

            q, k, v, gq, gk, gv, la, og = _inproj_even(yp, g4[0], w_parts, w_a2, b_alpha2[li])
            t = min(ATTN_TILE, t_len)
            bias = _prompt_bias(rel_table, t, t_len)
            sh = lambda x: x.reshape(bp, t_len, x.shape[-1])
            a_out = _attn_prompt(sh(q), sh(k), sh(v), lam_vecs, bias, g_diff[li], lam_init)
            o_gla, s_p = _gla_prompt(sh(gq), sh(gk), sh(gv), sh(la), sh(og), g_gla[li])
            yp = _post([a_out.reshape(bp * t_len, DA_V), o_gla.reshape(bp * t_len, GLA_V)], wo_parts, yp, g4, wu, wd)
            outs["kp"].append(k.reshape(bp, t_len, DA_HEADS, 2 * DA_DH))
            outs["vp"].append(v.reshape(bp, t_len, DA_HEADS, DA_VD))
            outs["gp"].append(s_p)

            q, k, v, gq, gk, gv, la, og = _inproj_even(ys, g4[0], w_parts, w_a2, b_alpha2[li])
            a_out = _attn_sample(q, k, v, cache_k, cache_v, li, page_table, lam_vecs, rel_table, g_diff[li], lam_init)
            o_gla, s_s = _gla_sample(gq, gk, gv, la, og, state_gla[li], g_gla[li])
            ys = _post([a_out, o_gla], wo_parts, ys, g4, wu, wd)
            outs["ks"].append(k.reshape(bd, 1, DA_HEADS, 2 * DA_DH))
            outs["vs"].append(v.reshape(bd, 1, DA_HEADS, DA_VD))
            outs["gs"].append(s_s)
        else:
            w_in = _mx(w_in_odd[li])
            d_rnn = w_in.shape[1] // 2
            w_gate, w_x = w_in[:, :d_rnn], w_in[:, d_rnn:]
            wga, wgx = _mx(w_gate_a[li]), _mx(w_gate_x[li])
            w_out = [_mx(w_out_odd[li])]
            rnn = (conv_w[li], conv_b[li], wga, b_gate_a[li], wgx, b_gate_x[li], rg_lambda[li])

            gate, x_br = _inproj_odd(yp, g4[0], w_gate, w_x)
            x_br3 = x_br.reshape(bp, t_len, d_rnn)
            y, h_last = _rglru_prompt(x_br3, gate.reshape(bp, t_len, d_rnn), *rnn)
            yp = _post([y.reshape(bp * t_len, d_rnn)], w_out, yp, g4, wu, wd)
            if t_len >= CONV_W - 1:
                new_buf = x_br3[:, t_len - (CONV_W - 1):]
            else:
                new_buf = jnp.concatenate([jnp.zeros((bp, CONV_W - 1 - t_len, d_rnn), F32), x_br3], axis=1)
            outs["cp"].append(new_buf)
            outs["rp"].append(h_last)

            gate, x_br = _inproj_odd(ys, g4[0], w_gate, w_x)
            y, h_new = _rglru_sample(x_br, gate, state_conv[li], state_rglru[li], *rnn)
            ys = _post([y], w_out, ys, g4, wu, wd)
            outs["cs"].append(jnp.concatenate([state_conv[li][:, 1:], x_br[:, None, :]], axis=1))
            outs["rs"].append(h_new)

    st = lambda key: jnp.stack(outs[key])
    return (yp.reshape(bp, t_len, d), ys.reshape(bd, 1, d), st("kp"), st("vp"), st("ks"), st("vs"),
            st("gp"), st("gs"), st("cp"), st("cs"), st("rp"), st("rs"))
```

```python
import functools
import math

import numpy as np
import jax
import jax.numpy as jnp
from jax import lax
from jax.experimental import pallas as pl
from jax.experimental.pallas import tpu as pltpu

DA_HEADS = 4
DA_DH = 64
DA_VD = 2 * DA_DH
GLA_HEADS = 4
GLA_DK = 64
GLA_DV = 128
GLA_RANK = 16
GLA_TAU = 16.0
DA_QK = DA_HEADS * 2 * DA_DH
DA_V = DA_HEADS * DA_VD
GLA_QK = GLA_HEADS * GLA_DK
GLA_V = GLA_HEADS * GLA_DV
SPLIT_SIZES = (DA_QK, DA_QK, DA_V, GLA_QK, GLA_QK, GLA_V, GLA_RANK, GLA_V)
RNN_BLOCKS = 8
CONV_W = 4
RG_C = 8.0
NUM_BUCKETS = 32
MAX_DISTANCE = 128
PAGE_SIZE = 128
EPS = 1e-6
NEG = -1e30

V7X_LANES = 128
V7X_SUBLANES = 8
V7X_VMEM_LIMIT_BYTES = 56 * 1024 * 1024

F32 = jnp.float32
_MXU_DTYPE = jnp.bfloat16

ROW_TILE = 512
FF_CHUNK = 1024
ATTN_Q_TILE = 256
ATTN_K_TILE = 128
ATTN_HEADS_PER_STEP = 4
GLA_CHUNK = 64
GLA_UNROLL = 2
RNN_TIME_TILE = 128
RNN_SEQS_PER_STEP = 4
PAGES_PER_STEP = 16


def _mx(x):
    return x.astype(_MXU_DTYPE)


def _dot(a, b):
    return jnp.dot(_mx(a), _mx(b), preferred_element_type=F32)


def _dot_nt(a, b):
    return lax.dot_general(_mx(a), _mx(b), (((1,), (1,)), ((), ())), preferred_element_type=F32)


def _dot_tn(a, b):
    return lax.dot_general(_mx(a), _mx(b), (((0,), (0,)), ((), ())), preferred_element_type=F32)


def _rms(x, g):
    return x * lax.rsqrt(jnp.mean(x * x, axis=-1, keepdims=True) + EPS) * g


def _softplus(x):
    return jnp.maximum(x, 0.0) + jnp.log1p(jnp.exp(-jnp.abs(x)))


def _log_sigmoid(x):
    return -_softplus(-x)


def _sigmoid_tanh(x):
    return 0.5 * jnp.tanh(0.5 * x) + 0.5


def _gelu_tanh(x):
    c = math.sqrt(2.0 / math.pi)
    return 0.5 * x * (1.0 + jnp.tanh(c * (x + 0.044715 * (x * x * x))))


def _params(*sem):
    return pltpu.CompilerParams(dimension_semantics=sem, vmem_limit_bytes=V7X_VMEM_LIMIT_BYTES)


def _const_spec(shape):
    nd = len(shape)
    return pl.BlockSpec(shape, lambda *_: (0,) * nd, pipeline_mode=pl.Buffered(1))


def _row_tile(m):
    return ROW_TILE if m % ROW_TILE == 0 else m


def _inproj_even_kernel(x_ref, g_ref, wq, wk, wv, wgq, wgk, wgv, wlr, wog, wa2, ba2,
                        q_o, k_o, v_o, gq_o, gk_o, gv_o, la_o, og_o):
    h = _mx(_rms(x_ref[...], g_ref[...]))
    q_o[...] = (_dot(h, wq[...]) * (DA_DH ** -0.5)).astype(q_o.dtype)
    k_o[...] = _dot(h, wk[...])
    v_o[...] = _dot(h, wv[...])
    gq_o[...] = _dot(h, wgq[...]) * (GLA_DK ** -0.5)
    gk_o[...] = _dot(h, wgk[...])
    gv_o[...] = _dot(h, wgv[...]).astype(gv_o.dtype)
    z = _dot(_dot(h, wlr[...]), wa2[...]) + ba2[...]
    la_o[...] = _log_sigmoid(z) * (1.0 / GLA_TAU)
    og = _dot(h, wog[...])
    og_o[...] = og * jax.nn.sigmoid(og)


def _inproj_even(x, g, w_parts, w_a2, b_a2):
    m, d = x.shape
    tm = _row_tile(m)
    out_cols = (DA_QK, DA_QK, DA_V, GLA_QK, GLA_QK, GLA_V, GLA_QK, GLA_V)
    out_dtypes = (_MXU_DTYPE, F32, F32, F32, F32, _MXU_DTYPE, F32, F32)
    in_specs = [pl.BlockSpec((tm, d), lambda i: (i, 0)), _const_spec((1, d))]
    in_specs += [_const_spec(w.shape) for w in w_parts]
    in_specs += [_const_spec(w_a2.shape), _const_spec((1, GLA_QK))]
    return pl.pallas_call(
        _inproj_even_kernel,
        grid=(m // tm,),
        in_specs=in_specs,
        out_specs=[pl.BlockSpec((tm, n), lambda i: (i, 0)) for n in out_cols],
        out_shape=[jax.ShapeDtypeStruct((m, n), dt) for n, dt in zip(out_cols, out_dtypes)],
        compiler_params=_params("parallel"),
        name="inproj_even",
    )(x, g.reshape(1, d), *w_parts, w_a2, b_a2.reshape(1, GLA_QK))


def _inproj_odd_kernel(x_ref, g_ref, wg, wx, gate_o, x_o):
    h = _mx(_rms(x_ref[...], g_ref[...]))
    gate_o[...] = _gelu_tanh(_dot(h, wg[...]))
    x_o[...] = _dot(h, wx[...])


def _inproj_odd(x, g, w_gate, w_x):
    m, d = x.shape
    tm = _row_tile(m)
    n = w_gate.shape[1]
    out_spec = pl.BlockSpec((tm, n), lambda i: (i, 0))
    out_shape = jax.ShapeDtypeStruct((m, n), F32)
    return pl.pallas_call(
        _inproj_odd_kernel,
        grid=(m // tm,),
        in_specs=[pl.BlockSpec((tm, d), lambda i: (i, 0)), _const_spec((1, d)),
                  _const_spec(w_gate.shape), _const_spec(w_x.shape)],
        out_specs=[out_spec] * 2,
        out_shape=[out_shape] * 2,
        compiler_params=_params("parallel"),
        name="inproj_odd",
    )(x, g.reshape(1, d), w_gate, w_x)


def _post_kernel(*refs, n_mix):
    mix_refs = refs[:n_mix]
    wo_refs = refs[n_mix:2 * n_mix]
    x_ref, g_ref, wup_ref, wdn_ref, y_ref = refs[2 * n_mix:]
    m = _dot(mix_refs[0][...], wo_refs[0][...])
    for mr, wr in zip(mix_refs[1:], wo_refs[1:]):
        m = m + _dot(mr[...], wr[...])
    g = g_ref[...]
    y1 = x_ref[...] + _rms(m, g[1:2])
    h = _mx(_rms(y1, g[2:3]))
    d_ff = wup_ref.shape[1]
    fc = min(FF_CHUNK, d_ff)
    acc = None
    for c in range(d_ff // fc):
        a = jnp.square(jnp.maximum(_dot(h, wup_ref[:, c * fc:(c + 1) * fc]), 0.0))
        part = _dot(a, wdn_ref[c * fc:(c + 1) * fc, :])
        acc = part if acc is None else acc + part
    y_ref[...] = y1 + _rms(acc, g[3:4])


def _post(mix_parts, wo_parts, x, g4, w_up, w_down):
    m, d = x.shape
    tm = _row_tile(m)
    n_mix = len(mix_parts)
    in_specs = [pl.BlockSpec((tm, p.shape[1]), lambda i: (i, 0)) for p in mix_parts]
    in_specs += [_const_spec(w.shape) for w in wo_parts]
    in_specs += [pl.BlockSpec((tm, d), lambda i: (i, 0)), _const_spec(g4.shape),
                 _const_spec(w_up.shape), _const_spec(w_down.shape)]
    return pl.pallas_call(
        functools.partial(_post_kernel, n_mix=n_mix),
        grid=(m // tm,),
        in_specs=in_specs,
        out_specs=pl.BlockSpec((tm, d), lambda i: (i, 0)),
        out_shape=jax.ShapeDtypeStruct((m, d), F32),
        compiler_params=_params("parallel"),
        name="post_mlp",
    )(*mix_parts, *wo_parts, x, g4, w_up, w_down)


def _t5_bucket(rel):
    rel = jnp.maximum(rel, 0)
    max_exact = NUM_BUCKETS // 2
    large = max_exact + (jnp.log(jnp.maximum(rel, 1).astype(F32) / max_exact)
                         / math.log(MAX_DISTANCE / max_exact) * (NUM_BUCKETS - max_exact)).astype(jnp.int32)
    large = jnp.minimum(large, NUM_BUCKETS - 1)
    return jnp.where(rel < max_exact, rel, large)


def _rel_bias(table, qpos, kpos):
    bucket = _t5_bucket(qpos[:, None] - kpos[None, :])
    tab = table.astype(F32)
    out = jnp.zeros((tab.shape[1],) + bucket.shape, F32)
    for bk in range(NUM_BUCKETS):
        out = jnp.where((bucket == bk)[None], tab[bk][:, None, None], out)
    return out


def _check_far_bucket_constant(first, last):
    rel = np.arange(first, last + 1, dtype=np.float64)
    max_exact = NUM_BUCKETS // 2
    steps = np.log(rel / max_exact) / math.log(MAX_DISTANCE / max_exact) * (NUM_BUCKETS - max_exact)
    assert first >= max_exact and np.all(steps >= NUM_BUCKETS - max_exact - 0.5), "far keys must share one bias bucket"


def _lambda_from(lam_ref, lam_init):
    lv = lam_ref[...]
    s1 = jnp.sum(lv[0:1] * lv[1:2], axis=-1, keepdims=True)
    s2 = jnp.sum(lv[2:3] * lv[3:4], axis=-1, keepdims=True)
    return jnp.exp(s1) - jnp.exp(s2) + lam_init


def _attn_prompt_kernel(lam_ref, q_ref, k_ref, v_ref, bias_ref, g_ref, o_ref,
                        kb, vt, m_s, l_s, acc_s, *, tq, tk, nh, lam_init):
    i = pl.program_id(2)
    n_blk = vt.shape[0]
    hd = 2 * DA_DH
    per_q = tq // tk

    @pl.when(i == 0)
    def _():
        kb[...] = _mx(k_ref[0])
        for jj in range(n_blk):
            for h in range(nh):
                vt[jj, h] = _mx(v_ref[0, jj * tk:(jj + 1) * tk, h * DA_VD:(h + 1) * DA_VD].T)

    sub = lax.broadcasted_iota(jnp.int32, (hd, tq), 0)
    zero = jnp.zeros((hd, tq), F32)
    q_t = []
    for h in range(nh):
        qh = q_ref[0, :, h * hd:(h + 1) * hd].astype(F32).T
        q_t.append((_mx(jnp.where(sub < DA_DH, qh, zero)), _mx(jnp.where(sub >= DA_DH, qh, zero))))

    m_s[...] = jnp.full(m_s.shape, -jnp.inf, F32)
    l_s[...] = jnp.zeros(l_s.shape, F32)
    acc_s[...] = jnp.zeros(acc_s.shape, F32)

    def step(j, tile):
        rows = pl.ds(pl.multiple_of(j * tk, tk), tk)
        for h in range(nh):
            k_blk = kb[rows, h * hd:(h + 1) * hd]
            v_blk = vt[j, h]
            for c in range(2):
                s = _dot(k_blk, q_t[h][c])
                if tile is not None:
                    s = s + bias_ref[h, tile]
                m_old = m_s[h, c]
                m_new = jnp.maximum(m_old, jnp.max(s, axis=0, keepdims=True))
                alpha = jnp.exp(m_old - m_new)
                p = jnp.exp(s - m_new)
                l_s[h, c] = alpha * l_s[h, c] + jnp.sum(p, axis=0, keepdims=True)
                acc_s[h, c] = alpha * acc_s[h, c] + _dot(v_blk, p)
                m_s[h, c] = m_new

    def far(jq, carry):
        for d in range(per_q):
            step(jq * per_q + d, None)
        return carry

    lax.fori_loop(0, jnp.maximum(i - 1, 0), far, 0)

    @pl.when(i > 0)
    def _():
        for d in range(per_q - 1):
            step((i - 1) * per_q + d, None)
        step(i * per_q - 1, 0)

    for d in range(per_q):
        step(i * per_q + d, 1 + d)

    lam = _lambda_from(lam_ref, lam_init)
    for h in range(nh):
        o = (acc_s[h, 0] / l_s[h, 0] - lam * (acc_s[h, 1] / l_s[h, 1])).T
        o_ref[0, :, h * DA_VD:(h + 1) * DA_VD] = (_rms(o, g_ref[...]) * (1.0 - lam_init)).astype(o_ref.dtype)


def _attn_prompt(q, k, v, lam_vecs, bias, g_da, lam_init):
    b, t_len, _ = q.shape
    tq, tk = _attn_tiles(t_len)
    nh = ATTN_HEADS_PER_STEP
    hd = 2 * DA_DH
    return pl.pallas_call(
        functools.partial(_attn_prompt_kernel, tq=tq, tk=tk, nh=nh, lam_init=lam_init),
        grid=(b, DA_HEADS // nh, t_len // tq),
        in_specs=[
            _const_spec(lam_vecs.shape),
            pl.BlockSpec((1, tq, nh * hd), lambda bi, h, i: (bi, i, h)),
            pl.BlockSpec((1, t_len, nh * hd), lambda bi, h, i: (bi, 0, h)),
            pl.BlockSpec((1, t_len, nh * DA_VD), lambda bi, h, i: (bi, 0, h)),
            pl.BlockSpec((nh,) + bias.shape[1:], lambda bi, h, i: (h, 0, 0, 0)),
            _const_spec((1, DA_VD)),
        ],
        out_specs=pl.BlockSpec((1, tq, nh * DA_VD), lambda bi, h, i: (bi, i, h)),
        out_shape=jax.ShapeDtypeStruct((b, t_len, DA_V), _MXU_DTYPE),
        scratch_shapes=[
            pltpu.VMEM((t_len, nh * hd), _MXU_DTYPE),
            pltpu.VMEM((t_len // tk, nh, DA_VD, tk), _MXU_DTYPE),
            pltpu.VMEM((nh, 2, 1, tq), F32),
            pltpu.VMEM((nh, 2, 1, tq), F32),
            pltpu.VMEM((nh, 2, DA_VD, tq), F32),
        ],
        compiler_params=_params("parallel", "parallel", "arbitrary"),
        name="attn_prompt",
    )(lam_vecs, q, k, v, bias, g_da.reshape(1, DA_VD))


def _attn_tiles(t_len):
    tq = min(ATTN_Q_TILE, t_len)
    tk = min(ATTN_K_TILE, tq)
    assert t_len % tq == 0 and tq % tk == 0
    return tq, tk


def _prompt_bias(rel_table, t_len):
    tq, tk = _attn_tiles(t_len)
    qpos = tk + jnp.arange(tq)
    kpos = jnp.arange(tk + tq)
    bias = _rel_bias(rel_table, qpos, kpos)
    if t_len > tq + tk:
        _check_far_bucket_constant(tk + 1, t_len)
    far = rel_table[NUM_BUCKETS - 1].astype(F32)
    bias = bias - far[:, None, None]
    mask = qpos[:, None] >= kpos[None, :]
    bias = jnp.where(mask[None], bias, NEG)
    return jnp.swapaxes(bias, 1, 2).reshape(bias.shape[0], 1 + tq // tk, tk, tq)


def _attn_sample_kernel(pt_ref, lam_ref, q_ref, kn_ref, vn_ref, bias_ref, biasn_ref, g_ref, *rest,
                        n_pg, lam_init):
    del pt_ref
    k_refs = rest[:n_pg]
    v_refs = rest[n_pg:2 * n_pg]
    o_ref, m_s, l_s, acc_s = rest[2 * n_pg:]
    step_i = pl.program_id(1)
    n_steps = pl.num_programs(1)
    rows8 = 2 * DA_HEADS
    hd = 2 * DA_DH
    pg_cols = PAGE_SIZE * DA_HEADS // 2

    @pl.when(step_i == 0)
    def _():
        m_s[...] = jnp.full(m_s.shape, -jnp.inf, F32)
        l_s[...] = jnp.zeros(l_s.shape, F32)
        acc_s[...] = jnp.zeros(acc_s.shape, F32)

    def rep2(x):
        return jnp.concatenate([x[r // 2:r // 2 + 1] for r in range(rows8)], axis=0)

    def side_by_side(x):
        even = (lax.broadcasted_iota(jnp.int32, x.shape, 0) // 2) % 2 == 0
        return jnp.concatenate([jnp.where(even, x, 0.0), jnp.where(even, 0.0, x)], axis=1)

    def page_pairs(ref):
        return jnp.concatenate([ref[0, pl.ds(0, pg_cols, stride=2), :], ref[0, pl.ds(1, pg_cols, stride=2), :]], axis=1)

    rowi = lax.broadcasted_iota(jnp.int32, (rows8, hd), 0)
    lanei = lax.broadcasted_iota(jnp.int32, (rows8, hd), 1)
    q8 = side_by_side(jnp.where(lanei // DA_DH == rowi % 2, rep2(q_ref[0].astype(F32)), 0.0))

    s = jnp.concatenate([_dot_nt(q8, page_pairs(kr)) for kr in k_refs], axis=1) + bias_ref[...]
    m_old = m_s[...]
    m_new = jnp.maximum(m_old, jnp.max(s, axis=-1, keepdims=True))
    alpha = jnp.exp(m_old - m_new)
    p = jnp.exp(s - m_new)
    l_new = alpha * l_s[...] + jnp.sum(p, axis=-1, keepdims=True)
    pv = _dot(p[:, 0:pg_cols], page_pairs(v_refs[0]))
    for gi in range(1, n_pg):
        pv = pv + _dot(p[:, gi * pg_cols:(gi + 1) * pg_cols], page_pairs(v_refs[gi]))
    acc_new = alpha * acc_s[...] + pv
    m_s[...] = m_new
    l_s[...] = l_new
    acc_s[...] = acc_new

    @pl.when(step_i == n_steps - 1)
    def _():
        kn = rep2(kn_ref[0])
        vn = rep2(vn_ref[0])
        sn = jnp.sum(q8 * jnp.concatenate([kn, kn], axis=1), axis=-1, keepdims=True) + biasn_ref[...]
        m_fin = jnp.maximum(m_new, sn)
        a2 = jnp.exp(m_new - m_fin)
        pn = jnp.exp(sn - m_fin)
        l_fin = a2 * l_new + pn
        acc_fin = a2 * acc_new + pn * jnp.concatenate([vn, vn], axis=1)
        an = acc_fin / l_fin
        lam = _lambda_from(lam_ref, lam_init)
        own = [slice((h % 2) * DA_VD, (h % 2 + 1) * DA_VD) for h in range(DA_HEADS)]
        o = jnp.concatenate([an[2 * h:2 * h + 1, own[h]] - lam * an[2 * h + 1:2 * h + 2, own[h]]
                             for h in range(DA_HEADS)], axis=0)
        o_ref[0] = (_rms(o, g_ref[...]) * (1.0 - lam_init)).astype(o_ref.dtype)


def _attn_sample(q, k_new, v_new, cache_k, cache_v, layer_idx, page_table, lam_vecs, rel_table, g_da, lam_init):
    bd = q.shape[0]
    n_pages = page_table.shape[1]
    past = n_pages * PAGE_SIZE
    n_pool = cache_k.shape[1]
    hd = 2 * DA_DH
    pg_rows = PAGE_SIZE * DA_HEADS
    n_pg = PAGES_PER_STEP if n_pages % PAGES_PER_STEP == 0 else 1
    n_steps = n_pages // n_pg
    ck = cache_k.reshape(cache_k.shape[0] * n_pool, pg_rows, hd)
    cv = cache_v.reshape(cache_v.shape[0] * n_pool, pg_rows, DA_VD)
    base = layer_idx * n_pool

    qpos = jnp.full((1,), past, jnp.int32)
    bias_p = _rel_bias(rel_table, qpos, jnp.arange(past))[:, 0, :]
    n_pair = DA_HEADS // 2
    own_pair = (jnp.arange(DA_HEADS)[:, None] // 2 == jnp.arange(n_pair)[None, :])[:, None, :]
    bias_p = jnp.where(own_pair, bias_p[:, :, None], NEG).reshape(DA_HEADS, past * n_pair)
    bias_p = jnp.repeat(bias_p, 2, axis=0)
    bias_n = jnp.repeat(_rel_bias(rel_table, qpos, qpos)[:, 0, :], 2, axis=0)

    def page_spec(gi):
        return pl.BlockSpec((1, pg_rows, hd), lambda b, s, pt: (base + pt[b, s * n_pg + gi], 0, 0))

    row3 = lambda b, s, pt: (b, 0, 0)
    const2 = lambda b, s, pt: (0, 0)
    in_specs = [
        pl.BlockSpec(lam_vecs.shape, const2),
        pl.BlockSpec((1, DA_HEADS, hd), row3),
        pl.BlockSpec((1, DA_HEADS, hd), row3),
        pl.BlockSpec((1, DA_HEADS, DA_VD), row3),
        pl.BlockSpec((2 * DA_HEADS, n_pg * pg_rows // 2), lambda b, s, pt: (0, s)),
        pl.BlockSpec((2 * DA_HEADS, 1), const2),
        pl.BlockSpec((1, DA_VD), const2),
    ]
    in_specs += [page_spec(gi) for gi in range(n_pg)] + [page_spec(gi) for gi in range(n_pg)]
    out = pl.pallas_call(
        functools.partial(_attn_sample_kernel, n_pg=n_pg, lam_init=lam_init),
        grid_spec=pltpu.PrefetchScalarGridSpec(
            num_scalar_prefetch=1,
            grid=(bd, n_steps),
            in_specs=in_specs,
            out_specs=pl.BlockSpec((1, DA_HEADS, DA_VD), row3),
            scratch_shapes=[
                pltpu.VMEM((2 * DA_HEADS, 1), F32),
                pltpu.VMEM((2 * DA_HEADS, 1), F32),
                pltpu.VMEM((2 * DA_HEADS, 2 * DA_VD), F32),
            ],
        ),
        out_shape=jax.ShapeDtypeStruct((bd, DA_HEADS, DA_VD), _MXU_DTYPE),
        compiler_params=_params("parallel", "arbitrary"),
        name="attn_sample",
    )(page_table, lam_vecs, q.reshape(bd, DA_HEADS, hd), k_new.reshape(bd, DA_HEADS, hd),
      v_new.reshape(bd, DA_HEADS, DA_VD), bias_p, bias_n, g_da.reshape(1, DA_VD), *([ck] * n_pg), *([cv] * n_pg))
    return out.reshape(bd, DA_V)


def _gla_prompt_kernel(q_ref, k_ref, v_ref, la_ref, og_ref, g_ref, o_ref, s_ref, st_s, *, chunk):
    t_len = q_ref.shape[1]
    c_sz = chunk
    pair_w = 2 * GLA_DK
    st_s[...] = jnp.zeros(st_s.shape, F32)
    r_i = lax.broadcasted_iota(jnp.int32, (c_sz, c_sz), 0)
    c_i = lax.broadcasted_iota(jnp.int32, (c_sz, c_sz), 1)
    tril = r_i >= c_i
    tril_f = tril.astype(F32)
    tril2 = jnp.concatenate([tril, tril], axis=0)
    lane = lax.broadcasted_iota(jnp.int32, (c_sz, pair_w), 1)
    first = lane < GLA_DK
    g = g_ref[...]

    def split_heads(x):
        zero = jnp.zeros_like(x)
        return jnp.concatenate([jnp.where(first, x, zero), jnp.where(first, zero, x)], axis=0)

    def body(c, carry):
        rows = pl.ds(pl.multiple_of(c * c_sz, c_sz), c_sz)
        la = la_ref[0, rows, :]
        bc = jnp.dot(tril_f, la, precision=lax.Precision.HIGHEST, preferred_element_type=F32)
        b_last = bc[c_sz - 1:c_sz, :]
        q = q_ref[0, rows, :]
        k = k_ref[0, rows, :]
        qe = q * jnp.exp(bc)
        ke = k * jnp.exp(-bc)
        kd = k * jnp.exp(b_last - bc)
        dec = jnp.exp(b_last)
        v = v_ref[0, rows, :]
        og = og_ref[0, rows, :]
        for p in range(GLA_HEADS // 2):
            ps = slice(p * pair_w, (p + 1) * pair_w)
            qs = _mx(split_heads(qe[:, ps]))
            att = jnp.where(tril2, _dot_nt(qs, ke[:, ps]), 0.0)
            st = st_s[p]
            inter = _dot_nt(qs, st)
            va = v[:, (2 * p) * GLA_DV:(2 * p + 1) * GLA_DV]
            vb = v[:, (2 * p + 1) * GLA_DV:(2 * p + 2) * GLA_DV]
            o_a = _dot(att[0:c_sz], va) + inter[0:c_sz]
            o_b = _dot(att[c_sz:2 * c_sz], vb) + inter[c_sz:2 * c_sz]
            upd = _dot_tn(jnp.concatenate([va, vb], axis=0), split_heads(kd[:, ps]))
            st_s[p] = dec[:, ps] * st + upd
            for hh, o_h in ((2 * p, o_a), (2 * p + 1, o_b)):
                cols = slice(hh * GLA_DV, (hh + 1) * GLA_DV)
                o_ref[0, rows, cols] = (_rms(o_h, g) * og[:, cols]).astype(o_ref.dtype)
        return carry

    lax.fori_loop(0, t_len // c_sz, body, 0, unroll=GLA_UNROLL)
    for p in range(GLA_HEADS // 2):
        s_t = st_s[p].T
        s_ref[0, 2 * p] = s_t[0:GLA_DK]
        s_ref[0, 2 * p + 1] = s_t[GLA_DK:2 * GLA_DK]


def _gla_prompt(gq, gk, gv, la, og, g_gla):
    b, t_len, _ = gq.shape
    chunk = GLA_CHUNK if t_len % GLA_CHUNK == 0 else t_len
    seq = lambda n: pl.BlockSpec((1, t_len, n), lambda bi: (bi, 0, 0))
    return pl.pallas_call(
        functools.partial(_gla_prompt_kernel, chunk=chunk),
        grid=(b,),
        in_specs=[seq(GLA_QK), seq(GLA_QK), seq(GLA_V), seq(GLA_QK), seq(GLA_V), _const_spec((1, GLA_DV))],
        out_specs=[seq(GLA_V), pl.BlockSpec((1, GLA_HEADS, GLA_DK, GLA_DV), lambda bi: (bi, 0, 0, 0))],
        out_shape=[jax.ShapeDtypeStruct((b, t_len, GLA_V), _MXU_DTYPE),
                   jax.ShapeDtypeStruct((b, GLA_HEADS, GLA_DK, GLA_DV), F32)],
        scratch_shapes=[pltpu.VMEM((GLA_HEADS // 2, GLA_DV, 2 * GLA_DK), F32)],
        compiler_params=_params("parallel"),
        name="gla_prompt",
    )(gq, gk, gv, la, og, g_gla.reshape(1, GLA_DV))


def _gla_sample_kernel(qc_ref, kc_ref, lac_ref, v_ref, og_ref, s0_ref, g_ref, o_ref, s_ref):
    g = g_ref[...]
    for h in range(GLA_HEADS):
        bc = lac_ref[0, h]
        qe = qc_ref[0, h] * jnp.exp(bc)
        ke = kc_ref[0, h] * jnp.exp(-bc)
        kd = kc_ref[0, h] * jnp.exp(bc - bc)
        v = v_ref[0, h].astype(F32)
        s0 = s0_ref[0, h]
        att = jnp.sum(qe * ke, axis=0, keepdims=True)
        o = att * v + jnp.sum(qe * s0, axis=0, keepdims=True)
        s_ref[0, h] = jnp.exp(bc) * s0 + kd * v
        o_ref[0, h] = (_rms(o, g) * og_ref[0, h]).astype(o_ref.dtype)


def _gla_sample(gq, gk, gv, la, og, s0, g_gla):
    bd = gq.shape[0]
    col = lambda x: x.reshape(bd, GLA_HEADS, GLA_DK, 1)
    row = lambda x: x.reshape(bd, GLA_HEADS, 1, GLA_DV)
    col_spec = pl.BlockSpec((1, GLA_HEADS, GLA_DK, 1), lambda b: (b, 0, 0, 0))
    row_spec = pl.BlockSpec((1, GLA_HEADS, 1, GLA_DV), lambda b: (b, 0, 0, 0))
    st_spec = pl.BlockSpec((1, GLA_HEADS, GLA_DK, GLA_DV), lambda b: (b, 0, 0, 0))
    o, s_new = pl.pallas_call(
        _gla_sample_kernel,
        grid=(bd,),
        in_specs=[col_spec, col_spec, col_spec, row_spec, row_spec, st_spec, _const_spec((1, GLA_DV))],
        out_specs=[row_spec, st_spec],
        out_shape=[jax.ShapeDtypeStruct((bd, GLA_HEADS, 1, GLA_DV), _MXU_DTYPE),
                   jax.ShapeDtypeStruct((bd, GLA_HEADS, GLA_DK, GLA_DV), F32)],
        compiler_params=_params("parallel"),
        name="gla_sample",
    )(col(gq), col(gk), col(la), row(gv), row(og), s0, g_gla.reshape(1, GLA_DV))
    return o.reshape(bd, GLA_V), s_new


def _rglru_coeffs(xc, wga_ref, bga, wgx_ref, bgx, lam):
    bw = wga_ref.shape[1]
    r_parts, i_parts = [], []
    for n in range(wga_ref.shape[0]):
        xb = _mx(xc[:, n * bw:(n + 1) * bw])
        r_parts.append(_dot(xb, wga_ref[n]))
        i_parts.append(_dot(xb, wgx_ref[n]))
    r = _sigmoid_tanh(jnp.concatenate(r_parts, axis=1) + bga)
    i = _sigmoid_tanh(jnp.concatenate(i_parts, axis=1) + bgx)
    log_a = (-RG_C * _softplus(-lam)) * r
    a = jnp.exp(log_a)
    b = jnp.sqrt(-jnp.tanh(log_a) * (a * a + 1.0)) * (i * xc)
    return a, b


def _rglru_prompt_kernel(x_ref, gate_ref, cw_ref, cb_ref, wga_ref, bga_ref, wgx_ref, bgx_ref, lam_ref,
                         y_ref, hl_ref, xs_s, h_s, a_s, b_s):
    nb, tc, d = x_ref.shape
    pad = xs_s.shape[1] - tc

    @pl.when(pl.program_id(1) == 0)
    def _():
        xs_s[:, 0:pad, :] = jnp.zeros((nb, pad, d), F32)
        h_s[...] = jnp.zeros(h_s.shape, F32)

    cw = cw_ref[...]
    for s in range(nb):
        x = x_ref[s]
        xs_s[s, pad:pad + tc, :] = x
        xc = cb_ref[...] + cw[CONV_W - 1:CONV_W] * x
        for j in range(CONV_W - 1):
            off = pad - (CONV_W - 1) + j
            xc = xc + cw[j:j + 1] * xs_s[s, off:off + tc, :]
        xs_s[s, 0:pad, :] = x[tc - pad:tc]
        a, b = _rglru_coeffs(xc, wga_ref, bga_ref[...], wgx_ref, bgx_ref[...], lam_ref[...])
        a_s[s] = a
        b_s[s] = b

    def body(t, hs):
        row = pl.ds(t, 1)
        new = []
        for s in range(nb):
            h = a_s[s, row, :] * hs[s] + b_s[s, row, :]
            b_s[s, row, :] = h
            new.append(h)
        return tuple(new)

    hs = lax.fori_loop(0, tc, body, tuple(h_s[s] for s in range(nb)), unroll=8)
    for s in range(nb):
        h_s[s] = hs[s]
        hl_ref[s] = hs[s]
        y_ref[s] = (gate_ref[s] * b_s[s]).astype(y_ref.dtype)


def _rglru_prompt(x_br, gate, conv_w, conv_b, w_ga, b_ga, w_gx, b_gx, lam):
    b, t_len, d = x_br.shape
    tc = RNN_TIME_TILE if t_len % RNN_TIME_TILE == 0 else t_len
    nb = RNN_SEQS_PER_STEP if b % RNN_SEQS_PER_STEP == 0 else 1
    assert tc >= V7X_SUBLANES
    vec = lambda x: x.reshape(1, d)
    seq = pl.BlockSpec((nb, tc, d), lambda bi, ti: (bi, ti, 0))
    y, h_last = pl.pallas_call(
        _rglru_prompt_kernel,
        grid=(b // nb, t_len // tc),
        in_specs=[seq, seq, _const_spec(conv_w.shape), _const_spec((1, d)),
                  _const_spec(w_ga.shape), _const_spec((1, d)), _const_spec(w_gx.shape), _const_spec((1, d)),
                  _const_spec((1, d))],
        out_specs=[seq, pl.BlockSpec((nb, 1, d), lambda bi, ti: (bi, 0, 0))],
        out_shape=[jax.ShapeDtypeStruct((b, t_len, d), _MXU_DTYPE), jax.ShapeDtypeStruct((b, 1, d), F32)],
        scratch_shapes=[pltpu.VMEM((nb, V7X_SUBLANES + tc, d), F32), pltpu.VMEM((nb, 1, d), F32),
                        pltpu.VMEM((nb, tc, d), F32), pltpu.VMEM((nb, tc, d), F32)],
        compiler_params=_params("parallel", "arbitrary"),
        name="rglru_prompt",
    )(x_br, gate, conv_w, vec(conv_b), w_ga, vec(b_ga), w_gx, vec(b_gx), vec(lam))
    return y, h_last.reshape(b, d)


def _rglru_sample_kernel(x_ref, gate_ref, c0_ref, c1_ref, c2_ref, h0_ref, cw_ref, cb_ref,
                         wga_ref, bga_ref, wgx_ref, bgx_ref, lam_ref, y_ref, h_ref):
    cw = cw_ref[...]
    xc = cb_ref[...] + cw[0:1] * c0_ref[...] + cw[1:2] * c1_ref[...] + cw[2:3] * c2_ref[...] + cw[3:4] * x_ref[...]
    a, b = _rglru_coeffs(xc, wga_ref, bga_ref[...], wgx_ref, bgx_ref[...], lam_ref[...])
    h = a * h0_ref[...] + b
    h_ref[...] = h
    y_ref[...] = (gate_ref[...] * h).astype(y_ref.dtype)


def _rglru_sample(x_br, gate, conv_state, h0, conv_w, conv_b, w_ga, b_ga, w_gx, b_gx, lam):
    bd, d = x_br.shape
    vec = lambda x: x.reshape(1, d)
    args = (x_br, gate, conv_state[:, 0], conv_state[:, 1], conv_state[:, 2], h0, conv_w, vec(conv_b),
            w_ga, vec(b_ga), w_gx, vec(b_gx), vec(lam))
    return pl.pallas_call(
        _rglru_sample_kernel,
        grid=(1,),
        in_specs=[_const_spec(a.shape) for a in args],
        out_specs=[pl.BlockSpec((bd, d), lambda i: (0, 0))] * 2,
        out_shape=[jax.ShapeDtypeStruct((bd, d), _MXU_DTYPE), jax.ShapeDtypeStruct((bd, d), F32)],
        compiler_params=_params("arbitrary"),
        name="rglru_sample",
    )(*args)


def kernel(x_prompt, x_sample, cache_k, cache_v, state_gla, state_conv, state_rglru, page_table, rel_table, norm_g, w_up, w_down, w_in_even, w_alpha2, b_alpha2, lam_q1, lam_k1, lam_q2, lam_k2, g_diff, g_gla, w_out_even, w_in_odd, conv_w, conv_b, w_gate_a, b_gate_a, w_gate_x, b_gate_x, rg_lambda, w_out_odd):
    bp, t_len, d = x_prompt.shape
    bd, dec_seq, _ = x_sample.shape
    assert dec_seq == 1, "the sample group decodes one token per sequence"
    depth = norm_g.shape[0]
    yp = x_prompt.reshape(bp * t_len, d)
    ys = x_sample.reshape(bd, d)
    offs = [int(o) for o in np.cumsum(SPLIT_SIZES)]
    outs = {k: [] for k in ("kp", "vp", "ks", "vs", "gp", "gs", "cp", "cs", "rp", "rs")}

    for layer in range(depth):
        g4 = norm_g[layer]
        li = layer // 2
        wu = _mx(w_up[layer])
        wd = _mx(w_down[layer])
        if layer % 2 == 0:
            lam_init = 0.8 - 0.6 * math.exp(-0.3 * layer)
            w_in = _mx(w_in_even[li])
            w_parts = [w_in[:, a:b] for a, b in zip([0] + offs[:-1], offs)]
            w_a2 = _mx(w_alpha2[li])
            lam_vecs = jnp.stack([lam_q1[li], lam_k1[li], lam_q2[li], lam_k2[li]]).astype(F32)
            w_out = _mx(w_out_even[li])
            wo_parts = [w_out[:DA_V], w_out[DA_V:]]

            q, k, v, gq, gk, gv, la, og = _inproj_even(yp, g4[0], w_parts, w_a2, b_alpha2[li])
            bias = _prompt_bias(rel_table, t_len)
            sh = lambda x: x.reshape(bp, t_len, x.shape[-1])
            a_out = _attn_prompt(sh(q), sh(k), sh(v), lam_vecs, bias, g_diff[li], lam_init)
            o_gla, s_p = _gla_prompt(sh(gq), sh(gk), sh(gv), sh(la), sh(og), g_gla[li])
            yp = _post([a_out.reshape(bp * t_len, DA_V), o_gla.reshape(bp * t_len, GLA_V)], wo_parts, yp, g4, wu, wd)
            outs["kp"].append(k.reshape(bp, t_len, DA_HEADS, 2 * DA_DH))
            outs["vp"].append(v.reshape(bp, t_len, DA_HEADS, DA_VD))
            outs["gp"].append(s_p)

            q, k, v, gq, gk, gv, la, og = _inproj_even(ys, g4[0], w_parts, w_a2, b_alpha2[li])
            a_out = _attn_sample(q, k, v, cache_k, cache_v, li, page_table, lam_vecs, rel_table, g_diff[li], lam_init)
            o_gla, s_s = _gla_sample(gq, gk, gv, la, og, state_gla[li], g_gla[li])
            ys = _post([a_out, o_gla], wo_parts, ys, g4, wu, wd)
            outs["ks"].append(k.reshape(bd, 1, DA_HEADS, 2 * DA_DH))
            outs["vs"].append(v.reshape(bd, 1, DA_HEADS, DA_VD))
            outs["gs"].append(s_s)
        else:
            w_in = _mx(w_in_odd[li])
            d_rnn = w_in.shape[1] // 2
            w_gate, w_x = w_in[:, :d_rnn], w_in[:, d_rnn:]
            wga, wgx = _mx(w_gate_a[li]), _mx(w_gate_x[li])
            w_out = [_mx(w_out_odd[li])]
            rnn = (conv_w[li], conv_b[li], wga, b_gate_a[li], wgx, b_gate_x[li], rg_lambda[li])

            gate, x_br = _inproj_odd(yp, g4[0], w_gate, w_x)
            x_br3 = x_br.reshape(bp, t_len, d_rnn)
            y, h_last = _rglru_prompt(x_br3, gate.reshape(bp, t_len, d_rnn), *rnn)
            yp = _post([y.reshape(bp * t_len, d_rnn)], w_out, yp, g4, wu, wd)
            assert t_len >= CONV_W - 1
            outs["cp"].append(x_br3[:, t_len - (CONV_W - 1):])
            outs["rp"].append(h_last)

            gate, x_br = _inproj_odd(ys, g4[0], w_gate, w_x)
            y, h_new = _rglru_sample(x_br, gate, state_conv[li], state_rglru[li], *rnn)
            ys = _post([y], w_out, ys, g4, wu, wd)
            outs["cs"].append(jnp.concatenate([state_conv[li][:, 1:], x_br[:, None, :]], axis=1))
            outs["rs"].append(h_new)

    st = lambda key: jnp.stack(outs[key])
    return (yp.reshape(bp, t_len, d), ys.reshape(bd, 1, d), st("kp"), st("vp"), st("ks"), st("vs"),
            st("gp"), st("gs"), st("cp"), st("cs"), st("rp"), st("rs"))
```

```python
import functools
import math

import numpy as np
import jax
import jax.numpy as jnp
from jax import lax
from jax.experimental import pallas as pl
from jax.experimental.pallas import tpu as pltpu

DA_HEADS = 4
DA_DH = 64
DA_VD = 2 * DA_DH
GLA_HEADS = 4
GLA_DK = 64
GLA_DV = 128
GLA_RANK = 16
GLA_TAU = 16.0
DA_QK = DA_HEADS * 2 * DA_DH
DA_V = DA_HEADS * DA_VD
GLA_QK = GLA_HEADS * GLA_DK
GLA_V = GLA_HEADS * GLA_DV
SPLIT_SIZES = (DA_QK, DA_QK, DA_V, GLA_QK, GLA_QK, GLA_V, GLA_RANK, GLA_V)
RNN_BLOCKS = 8
CONV_W = 4
RG_C = 8.0
NUM_BUCKETS = 32
MAX_DISTANCE = 128
PAGE_SIZE = 128
EPS = 1e-6
NEG = -1e30

V7X_LANES = 128
V7X_SUBLANES = 8
V7X_VMEM_LIMIT_BYTES = 56 * 1024 * 1024

F32 = jnp.float32
_MXU_DTYPE = jnp.bfloat16

ROW_TILE = 512
FF_CHUNK = 1024
ATTN_Q_TILE = 256
ATTN_K_TILE = 128
ATTN_HEADS_PER_STEP = 4
GLA_CHUNK = 64
GLA_SAMPLE_SEQS_PER_STEP = 8
GLA_UNROLL = 2
RNN_TIME_TILE = 128
RNN_SEQS_PER_STEP = 4
PAGES_PER_STEP = 16


def _mx(x):
    return x.astype(_MXU_DTYPE)


def _dot(a, b):
    return jnp.dot(_mx(a), _mx(b), preferred_element_type=F32)


def _dot_nt(a, b):
    return lax.dot_general(_mx(a), _mx(b), (((1,), (1,)), ((), ())), preferred_element_type=F32)


def _dot_tn(a, b):
    return lax.dot_general(_mx(a), _mx(b), (((0,), (0,)), ((), ())), preferred_element_type=F32)


def _rms(x, g):
    return x * lax.rsqrt(jnp.mean(x * x, axis=-1, keepdims=True) + EPS) * g


def _softplus(x):
    return jnp.maximum(x, 0.0) + jnp.log1p(jnp.exp(-jnp.abs(x)))


def _log_sigmoid(x):
    return -_softplus(-x)


def _sigmoid_tanh(x):
    return 0.5 * jnp.tanh(0.5 * x) + 0.5


def _gelu_tanh(x):
    c = math.sqrt(2.0 / math.pi)
    return 0.5 * x * (1.0 + jnp.tanh(c * (x + 0.044715 * (x * x * x))))


def _params(*sem):
    return pltpu.CompilerParams(dimension_semantics=sem, vmem_limit_bytes=V7X_VMEM_LIMIT_BYTES)


def _const_spec(shape):
    nd = len(shape)
    return pl.BlockSpec(shape, lambda *_: (0,) * nd, pipeline_mode=pl.Buffered(1))


def _row_tile(m):
    return ROW_TILE if m % ROW_TILE == 0 else m


def _inproj_even_kernel(x_ref, g_ref, wq, wk, wv, wgq, wgk, wgv, wlr, wog, wa2, ba2,
                        q_o, k_o, v_o, gq_o, gk_o, gv_o, la_o, og_o):
    h = _mx(_rms(x_ref[...], g_ref[...]))
    q_o[...] = (_dot(h, wq[...]) * (DA_DH ** -0.5)).astype(q_o.dtype)
    k = _dot(h, wk[...])
    v = _dot(h, wv[...])
    tm = k.shape[0]
    for hh in range(DA_HEADS):
        k_o[pl.ds(hh, tm, stride=DA_HEADS), :] = k[:, hh * 2 * DA_DH:(hh + 1) * 2 * DA_DH]
        v_o[pl.ds(hh, tm, stride=DA_HEADS), :] = v[:, hh * DA_VD:(hh + 1) * DA_VD]
    gq_o[...] = _dot(h, wgq[...]) * (GLA_DK ** -0.5)
    gk_o[...] = _dot(h, wgk[...])
    gv_o[...] = _dot(h, wgv[...]).astype(gv_o.dtype)
    z = _dot(_dot(h, wlr[...]), wa2[...]) + ba2[...]
    la_o[...] = _log_sigmoid(z) * (1.0 / GLA_TAU)
    og = _dot(h, wog[...])
    og_o[...] = og * jax.nn.sigmoid(og)


def _inproj_even(x, g, w_parts, w_a2, b_a2):
    m, d = x.shape
    tm = _row_tile(m)
    out_cols = (DA_QK, DA_QK, DA_V, GLA_QK, GLA_QK, GLA_V, GLA_QK, GLA_V)
    out_dtypes = (_MXU_DTYPE, F32, F32, F32, F32, _MXU_DTYPE, F32, F32)
    in_specs = [pl.BlockSpec((tm, d), lambda i: (i, 0)), _const_spec((1, d))]
    in_specs += [_const_spec(w.shape) for w in w_parts]
    in_specs += [_const_spec(w_a2.shape), _const_spec((1, GLA_QK))]
    return pl.pallas_call(
        _inproj_even_kernel,
        grid=(m // tm,),
        in_specs=in_specs,
        out_specs=[pl.BlockSpec((tm, n), lambda i: (i, 0)) if idx not in (1, 2)
                   else pl.BlockSpec((tm * DA_HEADS, n // DA_HEADS), lambda i: (i, 0))
                   for idx, n in enumerate(out_cols)],
        out_shape=[jax.ShapeDtypeStruct((m, n) if idx not in (1, 2) else (m * DA_HEADS, n // DA_HEADS), dt)
                   for idx, (n, dt) in enumerate(zip(out_cols, out_dtypes))],
        compiler_params=_params("parallel"),
        name="inproj_even",
    )(x, g.reshape(1, d), *w_parts, w_a2, b_a2.reshape(1, GLA_QK))


def _inproj_odd_kernel(x_ref, g_ref, wg, wx, gate_o, x_o):
    h = _mx(_rms(x_ref[...], g_ref[...]))
    gate_o[...] = _gelu_tanh(_dot(h, wg[...]))
    x_o[...] = _dot(h, wx[...])


def _inproj_odd(x, g, w_gate, w_x):
    m, d = x.shape
    tm = _row_tile(m)
    n = w_gate.shape[1]
    out_spec = pl.BlockSpec((tm, n), lambda i: (i, 0))
    out_shape = jax.ShapeDtypeStruct((m, n), F32)
    return pl.pallas_call(
        _inproj_odd_kernel,
        grid=(m // tm,),
        in_specs=[pl.BlockSpec((tm, d), lambda i: (i, 0)), _const_spec((1, d)),
                  _const_spec(w_gate.shape), _const_spec(w_x.shape)],
        out_specs=[out_spec] * 2,
        out_shape=[out_shape] * 2,
        compiler_params=_params("parallel"),
        name="inproj_odd",
    )(x, g.reshape(1, d), w_gate, w_x)


def _post_kernel(*refs, n_mix):
    mix_refs = refs[:n_mix]
    wo_refs = refs[n_mix:2 * n_mix]
    x_ref, g_ref, wup_ref, wdn_ref, y_ref = refs[2 * n_mix:]
    m = _dot(mix_refs[0][...], wo_refs[0][...])
    for mr, wr in zip(mix_refs[1:], wo_refs[1:]):
        m = m + _dot(mr[...], wr[...])
    g = g_ref[...]
    y1 = x_ref[...] + _rms(m, g[1:2])
    h = _mx(_rms(y1, g[2:3]))
    d_ff = wup_ref.shape[1]
    fc = min(FF_CHUNK, d_ff)
    acc = None
    for c in range(d_ff // fc):
        a = jnp.square(jnp.maximum(_dot(h, wup_ref[:, c * fc:(c + 1) * fc]), 0.0))
        part = _dot(a, wdn_ref[c * fc:(c + 1) * fc, :])
        acc = part if acc is None else acc + part
    y_ref[...] = y1 + _rms(acc, g[3:4])


def _post(mix_parts, wo_parts, x, g4, w_up, w_down):
    m, d = x.shape
    tm = _row_tile(m)
    n_mix = len(mix_parts)
    in_specs = [pl.BlockSpec((tm, p.shape[1]), lambda i: (i, 0)) for p in mix_parts]
    in_specs += [_const_spec(w.shape) for w in wo_parts]
    in_specs += [pl.BlockSpec((tm, d), lambda i: (i, 0)), _const_spec(g4.shape),
                 _const_spec(w_up.shape), _const_spec(w_down.shape)]
    return pl.pallas_call(
        functools.partial(_post_kernel, n_mix=n_mix),
        grid=(m // tm,),
        in_specs=in_specs,
        out_specs=pl.BlockSpec((tm, d), lambda i: (i, 0)),
        out_shape=jax.ShapeDtypeStruct((m, d), F32),
        compiler_params=_params("parallel"),
        name="post_mlp",
    )(*mix_parts, *wo_parts, x, g4, w_up, w_down)


def _t5_bucket(rel):
    rel = jnp.maximum(rel, 0)
    max_exact = NUM_BUCKETS // 2
    large = max_exact + (jnp.log(jnp.maximum(rel, 1).astype(F32) / max_exact)
                         / math.log(MAX_DISTANCE / max_exact) * (NUM_BUCKETS - max_exact)).astype(jnp.int32)
    large = jnp.minimum(large, NUM_BUCKETS - 1)
    return jnp.where(rel < max_exact, rel, large)


def _rel_bias(table, qpos, kpos):
    bucket = _t5_bucket(qpos[:, None] - kpos[None, :])
    return jnp.transpose(table[bucket], (2, 0, 1)).astype(F32)


def _check_far_bucket_constant(first, last):
    rel = np.arange(first, last + 1, dtype=np.float64)
    max_exact = NUM_BUCKETS // 2
    steps = np.log(rel / max_exact) / math.log(MAX_DISTANCE / max_exact) * (NUM_BUCKETS - max_exact)
    assert first >= max_exact and np.all(steps >= NUM_BUCKETS - max_exact - 0.5), "far keys must share one bias bucket"


def _lambda_from(lam_ref, lam_init):
    lv = lam_ref[...]
    s1 = jnp.sum(lv[0:1] * lv[1:2], axis=-1, keepdims=True)
    s2 = jnp.sum(lv[2:3] * lv[3:4], axis=-1, keepdims=True)
    return jnp.exp(s1) - jnp.exp(s2) + lam_init


def _attn_prompt_kernel(lam_ref, q_ref, k_ref, v_ref, bias_ref, g_ref, o_ref,
                        kb, vt, m_s, l_s, acc_s, *, tq, tk, nh, lam_init):
    i = pl.program_id(2)
    n_blk = vt.shape[0]
    hd = 2 * DA_DH
    per_q = tq // tk

    @pl.when(i == 0)
    def _():
        def prep(jj, carry):
            for h in range(nh):
                src = pl.ds(jj * (tk * nh) + h, tk, stride=nh)
                kb[pl.ds(pl.multiple_of(jj * tk, tk), tk), h * hd:(h + 1) * hd] = _mx(k_ref[0, src, :])
                vt[jj, h] = _mx(v_ref[0, src, :].T)
            return carry

        lax.fori_loop(0, n_blk, prep, 0)

    sub = lax.broadcasted_iota(jnp.int32, (hd, tq), 0)
    zero = jnp.zeros((hd, tq), F32)
    q_t = []
    for h in range(nh):
        qh = q_ref[0, :, h * hd:(h + 1) * hd].astype(F32).T
        q_t.append((_mx(jnp.where(sub < DA_DH, qh, zero)), _mx(jnp.where(sub >= DA_DH, qh, zero))))

    m_s[...] = jnp.full(m_s.shape, -jnp.inf, F32)
    l_s[...] = jnp.zeros(l_s.shape, F32)
    acc_s[...] = jnp.zeros(acc_s.shape, F32)

    def step(j, tile, q_lo=0):
        rows = pl.ds(pl.multiple_of(j * tk, tk), tk)
        qs = slice(q_lo, tq)
        for h in range(nh):
            k_blk = kb[rows, h * hd:(h + 1) * hd]
            v_blk = vt[j, h]
            for c in range(2):
                s = _dot(k_blk, q_t[h][c][:, qs])
                if tile is not None:
                    s = s + bias_ref[h, tile, :, qs]
                m_old = m_s[h, c, :, qs]
                m_new = jnp.maximum(m_old, jnp.max(s, axis=0, keepdims=True))
                alpha = jnp.exp(m_old - m_new)
                p = jnp.exp(s - m_new)
                l_s[h, c, :, qs] = alpha * l_s[h, c, :, qs] + jnp.sum(p, axis=0, keepdims=True)
                acc_s[h, c, :, qs] = alpha * acc_s[h, c, :, qs] + _dot(v_blk, p)
                m_s[h, c, :, qs] = m_new

    def far(jq, carry):
        for d in range(per_q):
            step(jq * per_q + d, None)
        return carry

    lax.fori_loop(0, jnp.maximum(i - 1, 0), far, 0)

    @pl.when(i > 0)
    def _():
        for d in range(per_q - 1):
            step((i - 1) * per_q + d, None)
        step(i * per_q - 1, 0)

    for d in range(per_q):
        step(i * per_q + d, 1 + d, q_lo=d * tk)

    lam = _lambda_from(lam_ref, lam_init)
    for h in range(nh):
        o = (acc_s[h, 0] / l_s[h, 0] - lam * (acc_s[h, 1] / l_s[h, 1])).T
        o_ref[0, :, h * DA_VD:(h + 1) * DA_VD] = (_rms(o, g_ref[...]) * (1.0 - lam_init)).astype(o_ref.dtype)


def _attn_prompt(q, k, v, lam_vecs, bias, g_da, lam_init):
    b, t_len, _ = q.shape
    tq, tk = _attn_tiles(t_len)
    nh = ATTN_HEADS_PER_STEP
    assert nh == DA_HEADS, "k / v arrive as (token, head) rows holding every head"
    hd = 2 * DA_DH
    return pl.pallas_call(
        functools.partial(_attn_prompt_kernel, tq=tq, tk=tk, nh=nh, lam_init=lam_init),
        grid=(b, DA_HEADS // nh, t_len // tq),
        in_specs=[
            _const_spec(lam_vecs.shape),
            pl.BlockSpec((1, tq, nh * hd), lambda bi, h, i: (bi, i, h)),
            pl.BlockSpec((1, t_len * nh, hd), lambda bi, h, i: (bi, 0, 0)),
            pl.BlockSpec((1, t_len * nh, DA_VD), lambda bi, h, i: (bi, 0, 0)),
            pl.BlockSpec((nh,) + bias.shape[1:], lambda bi, h, i: (h, 0, 0, 0)),
            _const_spec((1, DA_VD)),
        ],
        out_specs=pl.BlockSpec((1, tq, nh * DA_VD), lambda bi, h, i: (bi, i, h)),
        out_shape=jax.ShapeDtypeStruct((b, t_len, DA_V), _MXU_DTYPE),
        scratch_shapes=[
            pltpu.VMEM((t_len, nh * hd), _MXU_DTYPE),
            pltpu.VMEM((t_len // tk, nh, DA_VD, tk), _MXU_DTYPE),
            pltpu.VMEM((nh, 2, 1, tq), F32),
            pltpu.VMEM((nh, 2, 1, tq), F32),
            pltpu.VMEM((nh, 2, DA_VD, tq), F32),
        ],
        compiler_params=_params("parallel", "parallel", "arbitrary"),
        name="attn_prompt",
    )(lam_vecs, q, k, v, bias, g_da.reshape(1, DA_VD))


def _attn_tiles(t_len):
    tq = min(ATTN_Q_TILE, t_len)
    tk = min(ATTN_K_TILE, tq)
    assert t_len % tq == 0 and tq % tk == 0
    return tq, tk


def _prompt_bias(rel_table, t_len):
    tq, tk = _attn_tiles(t_len)
    if t_len > tq + tk:
        _check_far_bucket_constant(tk + 1, t_len)
    span = tk + tq
    period = span + tq
    idx = jnp.arange(period)
    dist = jnp.where(idx < span, tk - idx, tk + (period - idx))
    vec = rel_table[_t5_bucket(dist)].astype(F32) - rel_table[NUM_BUCKETS - 1].astype(F32)[None, :]
    vec = jnp.where((dist >= 0)[:, None], vec, NEG).T
    flat = jnp.tile(vec, (1, tq))[:, :tq * (period - 1)]
    bias = flat.reshape(vec.shape[0], tq, period - 1)[:, :, :span]
    return jnp.swapaxes(bias, 1, 2).reshape(bias.shape[0], 1 + tq // tk, tk, tq)


def _attn_sample_kernel(pt_ref, lam_ref, q_ref, kn_ref, vn_ref, bias_ref, biasn_ref, g_ref, *rest,
                        n_pg, lam_init):
    del pt_ref
    k_refs = rest[:n_pg]
    v_refs = rest[n_pg:2 * n_pg]
    o_ref, m_s, l_s, acc_s = rest[2 * n_pg:]
    step_i = pl.program_id(1)
    n_steps = pl.num_programs(1)
    rows8 = 2 * DA_HEADS
    hd = 2 * DA_DH
    pg_cols = PAGE_SIZE * DA_HEADS

    @pl.when(step_i == 0)
    def _():
        m_s[...] = jnp.full(m_s.shape, -jnp.inf, F32)
        l_s[...] = jnp.zeros(l_s.shape, F32)
        acc_s[...] = jnp.zeros(acc_s.shape, F32)

    def rep2(x):
        return jnp.concatenate([x[r // 2:r // 2 + 1] for r in range(rows8)], axis=0)

    rowi = lax.broadcasted_iota(jnp.int32, (rows8, hd), 0)
    lanei = lax.broadcasted_iota(jnp.int32, (rows8, hd), 1)
    q8 = jnp.where(lanei // DA_DH == rowi % 2, rep2(q_ref[0].astype(F32)), 0.0)

    s = jnp.concatenate([_dot_nt(q8, kr[0]) for kr in k_refs], axis=1) + bias_ref[...]
    m_old = m_s[...]
    m_new = jnp.maximum(m_old, jnp.max(s, axis=-1, keepdims=True))
    alpha = jnp.exp(m_old - m_new)
    p = jnp.exp(s - m_new)
    l_new = alpha * l_s[...] + jnp.sum(p, axis=-1, keepdims=True)
    pv = _dot(p[:, 0:pg_cols], v_refs[0][0])
    for gi in range(1, n_pg):
        pv = pv + _dot(p[:, gi * pg_cols:(gi + 1) * pg_cols], v_refs[gi][0])
    acc_new = alpha * acc_s[...] + pv
    m_s[...] = m_new
    l_s[...] = l_new
    acc_s[...] = acc_new

    @pl.when(step_i == n_steps - 1)
    def _():
        sn = jnp.sum(q8 * rep2(kn_ref[0]), axis=-1, keepdims=True) + biasn_ref[...]
        m_fin = jnp.maximum(m_new, sn)
        a2 = jnp.exp(m_new - m_fin)
        pn = jnp.exp(sn - m_fin)
        l_fin = a2 * l_new + pn
        acc_fin = a2 * acc_new + pn * rep2(vn_ref[0])
        an = acc_fin / l_fin
        lam = _lambda_from(lam_ref, lam_init)
        o = jnp.concatenate([an[2 * h:2 * h + 1] - lam * an[2 * h + 1:2 * h + 2] for h in range(DA_HEADS)], axis=0)
        o_ref[0] = (_rms(o, g_ref[...]) * (1.0 - lam_init)).astype(o_ref.dtype)


def _attn_sample(q, k_new, v_new, cache_k, cache_v, layer_idx, page_table, lam_vecs, rel_table, g_da, lam_init):
    bd = q.shape[0]
    n_pages = page_table.shape[1]
    past = n_pages * PAGE_SIZE
    n_pool = cache_k.shape[1]
    hd = 2 * DA_DH
    pg_rows = PAGE_SIZE * DA_HEADS
    n_pg = PAGES_PER_STEP if n_pages % PAGES_PER_STEP == 0 else 1
    n_steps = n_pages // n_pg
    ck = cache_k.reshape(cache_k.shape[0] * n_pool, pg_rows, hd)
    cv = cache_v.reshape(cache_v.shape[0] * n_pool, pg_rows, DA_VD)
    base = layer_idx * n_pool

    qpos = jnp.full((1,), past, jnp.int32)
    n_near = min(past, MAX_DISTANCE)
    bias_p = _rel_bias(rel_table, qpos, jnp.arange(past - n_near, past))[:, 0, :]
    if past > n_near:
        _check_far_bucket_constant(n_near + 1, past)
        far = jnp.broadcast_to(rel_table[NUM_BUCKETS - 1].astype(F32)[:, None], (DA_HEADS, past - n_near))
        bias_p = jnp.concatenate([far, bias_p], axis=1)
    col_head = jnp.arange(past * DA_HEADS) % DA_HEADS
    own_head = col_head[None, :] == jnp.arange(DA_HEADS)[:, None]
    bias_p = jnp.where(own_head, jnp.repeat(bias_p, DA_HEADS, axis=1), NEG)
    bias_p = jnp.repeat(bias_p, 2, axis=0)
    bias_n = jnp.repeat(_rel_bias(rel_table, qpos, qpos)[:, 0, :], 2, axis=0)

    def page_spec(gi):
        return pl.BlockSpec((1, pg_rows, hd), lambda b, s, pt: (base + pt[b, s * n_pg + gi], 0, 0))

    row3 = lambda b, s, pt: (b, 0, 0)
    const2 = lambda b, s, pt: (0, 0)
    in_specs = [
        pl.BlockSpec(lam_vecs.shape, const2),
        pl.BlockSpec((1, DA_HEADS, hd), row3),
        pl.BlockSpec((1, DA_HEADS, hd), row3),
        pl.BlockSpec((1, DA_HEADS, DA_VD), row3),
        pl.BlockSpec((2 * DA_HEADS, n_pg * pg_rows), lambda b, s, pt: (0, s)),
        pl.BlockSpec((2 * DA_HEADS, 1), const2),
        pl.BlockSpec((1, DA_VD), const2),
    ]
    in_specs += [page_spec(gi) for gi in range(n_pg)] + [page_spec(gi) for gi in range(n_pg)]
    out = pl.pallas_call(
        functools.partial(_attn_sample_kernel, n_pg=n_pg, lam_init=lam_init),
        grid_spec=pltpu.PrefetchScalarGridSpec(
            num_scalar_prefetch=1,
            grid=(bd, n_steps),
            in_specs=in_specs,
            out_specs=pl.BlockSpec((1, DA_HEADS, DA_VD), row3),
            scratch_shapes=[
                pltpu.VMEM((2 * DA_HEADS, 1), F32),
                pltpu.VMEM((2 * DA_HEADS, 1), F32),
                pltpu.VMEM((2 * DA_HEADS, DA_VD), F32),
            ],
        ),
        out_shape=jax.ShapeDtypeStruct((bd, DA_HEADS, DA_VD), _MXU_DTYPE),
        compiler_params=_params("parallel", "arbitrary"),
        name="attn_sample",
    )(page_table, lam_vecs, q.reshape(bd, DA_HEADS, hd), k_new.reshape(bd, DA_HEADS, hd),
      v_new.reshape(bd, DA_HEADS, DA_VD), bias_p, bias_n, g_da.reshape(1, DA_VD), *([ck] * n_pg), *([cv] * n_pg))
    return out.reshape(bd, DA_V)


def _gla_prompt_kernel(q_ref, k_ref, v_ref, la_ref, og_ref, g_ref, o_ref, s_ref, st_s, *, chunk):
    t_len = q_ref.shape[1]
    c_sz = chunk
    pair_w = 2 * GLA_DK
    st_s[...] = jnp.zeros(st_s.shape, F32)
    r_i = lax.broadcasted_iota(jnp.int32, (c_sz, c_sz), 0)
    c_i = lax.broadcasted_iota(jnp.int32, (c_sz, c_sz), 1)
    tril = r_i >= c_i
    tril_f = tril.astype(F32)
    tril2 = jnp.concatenate([tril, tril], axis=0)
    lane = lax.broadcasted_iota(jnp.int32, (c_sz, pair_w), 1)
    first = lane < GLA_DK
    g = g_ref[...]

    def split_heads(x):
        zero = jnp.zeros_like(x)
        return jnp.concatenate([jnp.where(first, x, zero), jnp.where(first, zero, x)], axis=0)

    def body(c, carry):
        rows = pl.ds(pl.multiple_of(c * c_sz, c_sz), c_sz)
        la = la_ref[0, rows, :]
        bc = jnp.dot(tril_f, la, precision=lax.Precision.HIGHEST, preferred_element_type=F32)
        b_last = bc[c_sz - 1:c_sz, :]
        q = q_ref[0, rows, :]
        k = k_ref[0, rows, :]
        qe = q * jnp.exp(bc)
        ke = k * jnp.exp(-bc)
        kd = k * jnp.exp(b_last - bc)
        dec = jnp.exp(b_last)
        v = v_ref[0, rows, :]
        og = og_ref[0, rows, :]
        for p in range(GLA_HEADS // 2):
            ps = slice(p * pair_w, (p + 1) * pair_w)
            qs = _mx(split_heads(qe[:, ps]))
            att = jnp.where(tril2, _dot_nt(qs, ke[:, ps]), 0.0)
            st = st_s[p]
            inter = _dot_nt(qs, st)
            va = v[:, (2 * p) * GLA_DV:(2 * p + 1) * GLA_DV]
            vb = v[:, (2 * p + 1) * GLA_DV:(2 * p + 2) * GLA_DV]
            o_a = _dot(att[0:c_sz], va) + inter[0:c_sz]
            o_b = _dot(att[c_sz:2 * c_sz], vb) + inter[c_sz:2 * c_sz]
            upd = _dot_tn(jnp.concatenate([va, vb], axis=0), split_heads(kd[:, ps]))
            st_s[p] = dec[:, ps] * st + upd
            for hh, o_h in ((2 * p, o_a), (2 * p + 1, o_b)):
                cols = slice(hh * GLA_DV, (hh + 1) * GLA_DV)
                o_ref[0, rows, cols] = (_rms(o_h, g) * og[:, cols]).astype(o_ref.dtype)
        return carry

    lax.fori_loop(0, t_len // c_sz, body, 0, unroll=GLA_UNROLL)
    for p in range(GLA_HEADS // 2):
        s_t = st_s[p].T
        s_ref[0, 2 * p] = s_t[0:GLA_DK]
        s_ref[0, 2 * p + 1] = s_t[GLA_DK:2 * GLA_DK]


def _gla_prompt(gq, gk, gv, la, og, g_gla):
    b, t_len, _ = gq.shape
    chunk = GLA_CHUNK if t_len % GLA_CHUNK == 0 else t_len
    seq = lambda n: pl.BlockSpec((1, t_len, n), lambda bi: (bi, 0, 0))
    return pl.pallas_call(
        functools.partial(_gla_prompt_kernel, chunk=chunk),
        grid=(b,),
        in_specs=[seq(GLA_QK), seq(GLA_QK), seq(GLA_V), seq(GLA_QK), seq(GLA_V), _const_spec((1, GLA_DV))],
        out_specs=[seq(GLA_V), pl.BlockSpec((1, GLA_HEADS, GLA_DK, GLA_DV), lambda bi: (bi, 0, 0, 0))],
        out_shape=[jax.ShapeDtypeStruct((b, t_len, GLA_V), _MXU_DTYPE),
                   jax.ShapeDtypeStruct((b, GLA_HEADS, GLA_DK, GLA_DV), F32)],
        scratch_shapes=[pltpu.VMEM((GLA_HEADS // 2, GLA_DV, 2 * GLA_DK), F32)],
        compiler_params=_params("parallel"),
        name="gla_prompt",
    )(gq, gk, gv, la, og, g_gla.reshape(1, GLA_DV))


def _gla_sample_kernel(qc_ref, kc_ref, lac_ref, v_ref, og_ref, s0_ref, g_ref, o_ref, s_ref):
    g = g_ref[...]
    for bi in range(qc_ref.shape[0]):
        for h in range(GLA_HEADS):
            bc = lac_ref[bi, h]
            qe = qc_ref[bi, h] * jnp.exp(bc)
            ke = kc_ref[bi, h] * jnp.exp(-bc)
            kd = kc_ref[bi, h] * jnp.exp(bc - bc)
            v = v_ref[bi, h].astype(F32)
            s0 = s0_ref[bi, h]
            att = jnp.sum(qe * ke, axis=0, keepdims=True)
            o = att * v + jnp.sum(qe * s0, axis=0, keepdims=True)
            s_ref[bi, h] = jnp.exp(bc) * s0 + kd * v
            o_ref[bi, h] = (_rms(o, g) * og_ref[bi, h]).astype(o_ref.dtype)


def _gla_sample(gq, gk, gv, la, og, s0, g_gla):
    bd = gq.shape[0]
    nb = GLA_SAMPLE_SEQS_PER_STEP if bd % GLA_SAMPLE_SEQS_PER_STEP == 0 else 1
    col = lambda x: x.reshape(bd, GLA_HEADS, GLA_DK, 1)
    row = lambda x: x.reshape(bd, GLA_HEADS, 1, GLA_DV)
    col_spec = pl.BlockSpec((nb, GLA_HEADS, GLA_DK, 1), lambda b: (b, 0, 0, 0))
    row_spec = pl.BlockSpec((nb, GLA_HEADS, 1, GLA_DV), lambda b: (b, 0, 0, 0))
    st_spec = pl.BlockSpec((nb, GLA_HEADS, GLA_DK, GLA_DV), lambda b: (b, 0, 0, 0))
    o, s_new = pl.pallas_call(
        _gla_sample_kernel,
        grid=(bd // nb,),
        in_specs=[col_spec, col_spec, col_spec, row_spec, row_spec, st_spec, _const_spec((1, GLA_DV))],
        out_specs=[row_spec, st_spec],
        out_shape=[jax.ShapeDtypeStruct((bd, GLA_HEADS, 1, GLA_DV), _MXU_DTYPE),
                   jax.ShapeDtypeStruct((bd, GLA_HEADS, GLA_DK, GLA_DV), F32)],
        compiler_params=_params("parallel"),
        name="gla_sample",
    )(col(gq), col(gk), col(la), row(gv), row(og), s0, g_gla.reshape(1, GLA_DV))
    return o.reshape(bd, GLA_V), s_new


def _rglru_coeffs(xc, wga_ref, bga, wgx_ref, bgx, lam):
    bw = wga_ref.shape[1]
    r_parts, i_parts = [], []
    for n in range(wga_ref.shape[0]):
        xb = _mx(xc[:, n * bw:(n + 1) * bw])
        r_parts.append(_dot(xb, wga_ref[n]))
        i_parts.append(_dot(xb, wgx_ref[n]))
    r = _sigmoid_tanh(jnp.concatenate(r_parts, axis=1) + bga)
    i = _sigmoid_tanh(jnp.concatenate(i_parts, axis=1) + bgx)
    log_a = (-RG_C * _softplus(-lam)) * r
    a = jnp.exp(log_a)
    b = jnp.sqrt(-jnp.tanh(log_a) * (a * a + 1.0)) * (i * xc)
    return a, b


def _rglru_prompt_kernel(x_ref, gate_ref, cw_ref, cb_ref, wga_ref, bga_ref, wgx_ref, bgx_ref, lam_ref,
                         y_ref, hl_ref, xs_s, h_s, a_s, b_s):
    nb, tc, d = x_ref.shape
    pad = xs_s.shape[1] - tc

    @pl.when(pl.program_id(1) == 0)
    def _():
        xs_s[:, 0:pad, :] = jnp.zeros((nb, pad, d), F32)
        h_s[...] = jnp.zeros(h_s.shape, F32)

    cw = cw_ref[...]
    for s in range(nb):
        x = x_ref[s]
        xs_s[s, pad:pad + tc, :] = x
        xc = cb_ref[...] + cw[CONV_W - 1:CONV_W] * x
        for j in range(CONV_W - 1):
            off = pad - (CONV_W - 1) + j
            xc = xc + cw[j:j + 1] * xs_s[s, off:off + tc, :]
        xs_s[s, 0:pad, :] = x[tc - pad:tc]
        a, b = _rglru_coeffs(xc, wga_ref, bga_ref[...], wgx_ref, bgx_ref[...], lam_ref[...])
        a_s[s] = a
        b_s[s] = b

    def body(t, hs):
        row = pl.ds(t, 1)
        new = []
        for s in range(nb):
            h = a_s[s, row, :] * hs[s] + b_s[s, row, :]
            b_s[s, row, :] = h
            new.append(h)
        return tuple(new)

    hs = lax.fori_loop(0, tc, body, tuple(h_s[s] for s in range(nb)), unroll=8)
    for s in range(nb):
        h_s[s] = hs[s]
        hl_ref[s] = hs[s]
        y_ref[s] = (gate_ref[s] * b_s[s]).astype(y_ref.dtype)


def _rglru_prompt(x_br, gate, conv_w, conv_b, w_ga, b_ga, w_gx, b_gx, lam):
    b, t_len, d = x_br.shape
    tc = RNN_TIME_TILE if t_len % RNN_TIME_TILE == 0 else t_len
    nb = RNN_SEQS_PER_STEP if b % RNN_SEQS_PER_STEP == 0 else 1
    assert tc >= V7X_SUBLANES
    vec = lambda x: x.reshape(1, d)
    seq = pl.BlockSpec((nb, tc, d), lambda bi, ti: (bi, ti, 0))
    y, h_last = pl.pallas_call(
        _rglru_prompt_kernel,
        grid=(b // nb, t_len // tc),
        in_specs=[seq, seq, _const_spec(conv_w.shape), _const_spec((1, d)),
                  _const_spec(w_ga.shape), _const_spec((1, d)), _const_spec(w_gx.shape), _const_spec((1, d)),
                  _const_spec((1, d))],
        out_specs=[seq, pl.BlockSpec((nb, 1, d), lambda bi, ti: (bi, 0, 0))],
        out_shape=[jax.ShapeDtypeStruct((b, t_len, d), _MXU_DTYPE), jax.ShapeDtypeStruct((b, 1, d), F32)],
        scratch_shapes=[pltpu.VMEM((nb, V7X_SUBLANES + tc, d), F32), pltpu.VMEM((nb, 1, d), F32),
                        pltpu.VMEM((nb, tc, d), F32), pltpu.VMEM((nb, tc, d), F32)],
        compiler_params=_params("parallel", "arbitrary"),
        name="rglru_prompt",
    )(x_br, gate, conv_w, vec(conv_b), w_ga, vec(b_ga), w_gx, vec(b_gx), vec(lam))
    return y, h_last.reshape(b, d)


def _rglru_sample_kernel(x_ref, gate_ref, c0_ref, c1_ref, c2_ref, h0_ref, cw_ref, cb_ref,
                         wga_ref, bga_ref, wgx_ref, bgx_ref, lam_ref, y_ref, h_ref):
    cw = cw_ref[...]
    xc = cb_ref[...] + cw[0:1] * c0_ref[...] + cw[1:2] * c1_ref[...] + cw[2:3] * c2_ref[...] + cw[3:4] * x_ref[...]
    a, b = _rglru_coeffs(xc, wga_ref, bga_ref[...], wgx_ref, bgx_ref[...], lam_ref[...])
    h = a * h0_ref[...] + b
    h_ref[...] = h
    y_ref[...] = (gate_ref[...] * h).astype(y_ref.dtype)


def _rglru_sample(x_br, gate, conv_state, h0, conv_w, conv_b, w_ga, b_ga, w_gx, b_gx, lam):
    bd, d = x_br.shape
    vec = lambda x: x.reshape(1, d)
    args = (x_br, gate, conv_state[:, 0], conv_state[:, 1], conv_state[:, 2], h0, conv_w, vec(conv_b),
            w_ga, vec(b_ga), w_gx, vec(b_gx), vec(lam))
    return pl.pallas_call(
        _rglru_sample_kernel,
        grid=(1,),
        in_specs=[_const_spec(a.shape) for a in args],
        out_specs=[pl.BlockSpec((bd, d), lambda i: (0, 0))] * 2,
        out_shape=[jax.ShapeDtypeStruct((bd, d), _MXU_DTYPE), jax.ShapeDtypeStruct((bd, d), F32)],
        compiler_params=_params("arbitrary"),
        name="rglru_sample",
    )(*args)


def kernel(x_prompt, x_sample, cache_k, cache_v, state_gla, state_conv, state_rglru, page_table, rel_table, norm_g, w_up, w_down, w_in_even, w_alpha2, b_alpha2, lam_q1, lam_k1, lam_q2, lam_k2, g_diff, g_gla, w_out_even, w_in_odd, conv_w, conv_b, w_gate_a, b_gate_a, w_gate_x, b_gate_x, rg_lambda, w_out_odd):
    bp, t_len, d = x_prompt.shape
    bd, dec_seq, _ = x_sample.shape
    assert dec_seq == 1, "the sample group decodes one token per sequence"
    depth = norm_g.shape[0]
    yp = x_prompt.reshape(bp * t_len, d)
    ys = x_sample.reshape(bd, d)
    offs = [int(o) for o in np.cumsum(SPLIT_SIZES)]
    outs = {k: [] for k in ("kp", "vp", "ks", "vs", "gp", "gs", "cp", "cs", "rp", "rs")}

    for layer in range(depth):
        g4 = norm_g[layer]
        li = layer // 2
        wu = _mx(w_up[layer])
        wd = _mx(w_down[layer])
        if layer % 2 == 0:
            lam_init = 0.8 - 0.6 * math.exp(-0.3 * layer)
            w_in = _mx(w_in_even[li])
            w_parts = [w_in[:, a:b] for a, b in zip([0] + offs[:-1], offs)]
            w_a2 = _mx(w_alpha2[li])
            lam_vecs = jnp.stack([lam_q1[li], lam_k1[li], lam_q2[li], lam_k2[li]]).astype(F32)
            w_out = _mx(w_out_even[li])
            wo_parts = [w_out[:DA_V], w_out[DA_V:]]

            q, k, v, gq, gk, gv, la, og = _inproj_even(yp, g4[0], w_parts, w_a2, b_alpha2[li])
            bias = _prompt_bias(rel_table, t_len)
            sh = lambda x: x.reshape(bp, -1, x.shape[-1])
            a_out = _attn_prompt(sh(q), sh(k), sh(v), lam_vecs, bias, g_diff[li], lam_init)
            o_gla, s_p = _gla_prompt(sh(gq), sh(gk), sh(gv), sh(la), sh(og), g_gla[li])
            yp = _post([a_out.reshape(bp * t_len, DA_V), o_gla.reshape(bp * t_len, GLA_V)], wo_parts, yp, g4, wu, wd)
            outs["kp"].append(k.reshape(bp, t_len, DA_HEADS, 2 * DA_DH))
            outs["vp"].append(v.reshape(bp, t_len, DA_HEADS, DA_VD))
            outs["gp"].append(s_p)

            q, k, v, gq, gk, gv, la, og = _inproj_even(ys, g4[0], w_parts, w_a2, b_alpha2[li])
            a_out = _attn_sample(q, k, v, cache_k, cache_v, li, page_table, lam_vecs, rel_table, g_diff[li], lam_init)
            o_gla, s_s = _gla_sample(gq, gk, gv, la, og, state_gla[li], g_gla[li])
            ys = _post([a_out, o_gla], wo_parts, ys, g4, wu, wd)
            outs["ks"].append(k.reshape(bd, 1, DA_HEADS, 2 * DA_DH))
            outs["vs"].append(v.reshape(bd, 1, DA_HEADS, DA_VD))
            outs["gs"].append(s_s)
        else:
            w_in = _mx(w_in_odd[li])
            d_rnn = w_in.shape[1] // 2
            w_gate, w_x = w_in[:, :d_rnn], w_in[:, d_rnn:]
            wga, wgx = _mx(w_gate_a[li]), _mx(w_gate_x[li])
            w_out = [_mx(w_out_odd[li])]
            rnn = (conv_w[li], conv_b[li], wga, b_gate_a[li], wgx, b_gate_x[li], rg_lambda[li])

            gate, x_br = _inproj_odd(yp, g4[0], w_gate, w_x)
            x_br3 = x_br.reshape(bp, t_len, d_rnn)
            y, h_last = _rglru_prompt(x_br3, gate.reshape(bp, t_len, d_rnn), *rnn)
            yp = _post([y.reshape(bp * t_len, d_rnn)], w_out, yp, g4, wu, wd)
            assert t_len >= CONV_W - 1
            outs["cp"].append(x_br3[:, t_len - (CONV_W - 1):])
            outs["rp"].append(h_last)

            gate, x_br = _inproj_odd(ys, g4[0], w_gate, w_x)
            y, h_new = _rglru_sample(x_br, gate, state_conv[li], state_rglru[li], *rnn)
            ys = _post([y], w_out, ys, g4, wu, wd)
            outs["cs"].append(jnp.concatenate([state_conv[li][:, 1:], x_br[:, None, :]], axis=1))
            outs["rs"].append(h_new)

    st = lambda key: jnp.stack(outs[key])
    return (yp.reshape(bp, t_len, d), ys.reshape(bd, 1, d), st("kp"), st("vp"), st("ks"), st("vs"),
            st("gp"), st("gs"), st("cp"), st("cs"), st("rp"), st("rs"))
```

```python
import functools
import math

import numpy as np
import jax
import jax.numpy as jnp
from jax import lax
from jax.experimental import pallas as pl
from jax.experimental.pallas import tpu as pltpu

DA_HEADS = 4
DA_DH = 64
DA_VD = 2 * DA_DH
GLA_HEADS = 4
GLA_DK = 64
GLA_DV = 128
GLA_RANK = 16
GLA_TAU = 16.0
DA_QK = DA_HEADS * 2 * DA_DH
DA_V = DA_HEADS * DA_VD
GLA_QK = GLA_HEADS * GLA_DK
GLA_V = GLA_HEADS * GLA_DV
SPLIT_SIZES = (DA_QK, DA_QK, DA_V, GLA_QK, GLA_QK, GLA_V, GLA_RANK, GLA_V)
RNN_BLOCKS = 8
CONV_W = 4
RG_C = 8.0
NUM_BUCKETS = 32
MAX_DISTANCE = 128
PAGE_SIZE = 128
EPS = 1e-6
NEG = -1e30
LOG2E = math.log2(math.e)

V7X_LANES = 128
V7X_SUBLANES = 8
V7X_VMEM_LIMIT_BYTES = 56 * 1024 * 1024

F32 = jnp.float32
_MXU_DTYPE = jnp.bfloat16

ROW_TILE = 512
INPROJ_ROW_TILE = 1024
FF_CHUNK = 1024
ATTN_Q_TILE = 256
ATTN_K_TILE = 128
ATTN_HEADS_PER_STEP = 4
GLA_CHUNK = 64
GLA_SAMPLE_SEQS_PER_STEP = 8
GLA_UNROLL = 2
RNN_TIME_TILE = 128
RNN_SEQS_PER_STEP = 4
PAGES_PER_STEP = 32


def _mx(x):
    return x.astype(_MXU_DTYPE)


def _dot(a, b):
    return jnp.dot(_mx(a), _mx(b), preferred_element_type=F32)


def _dot_nt(a, b):
    return lax.dot_general(_mx(a), _mx(b), (((1,), (1,)), ((), ())), preferred_element_type=F32)


def _dot_tn(a, b):
    return lax.dot_general(_mx(a), _mx(b), (((0,), (0,)), ((), ())), preferred_element_type=F32)


def _rms(x, g):
    return x * lax.rsqrt(jnp.mean(x * x, axis=-1, keepdims=True) + EPS) * g


def _softplus(x):
    return jnp.maximum(x, 0.0) + jnp.log1p(jnp.exp(-jnp.abs(x)))


def _log_sigmoid(x):
    return -_softplus(-x)


def _sigmoid_tanh(x):
    return 0.5 * jnp.tanh(0.5 * x) + 0.5


def _gelu_tanh(x):
    c = math.sqrt(2.0 / math.pi)
    return 0.5 * x * (1.0 + jnp.tanh(c * (x + 0.044715 * (x * x * x))))


def _params(*sem):
    return pltpu.CompilerParams(dimension_semantics=sem, vmem_limit_bytes=V7X_VMEM_LIMIT_BYTES)


def _const_spec(shape):
    nd = len(shape)
    return pl.BlockSpec(shape, lambda *_: (0,) * nd, pipeline_mode=pl.Buffered(1))


def _row_tile(m, tile=None):
    tile = ROW_TILE if tile is None else tile
    return tile if m % tile == 0 else m


def _inproj_even_kernel(x_ref, g_ref, wq, wk, wv, wgq, wgk, wgv, wlr, wog, wa2, ba2,
                        q_o, k_o, v_o, gq_o, gk_o, gv_o, la_o, og_o):
    h = _mx(_rms(x_ref[...], g_ref[...]))
    q_o[...] = (_dot(h, wq[...]) * (DA_DH ** -0.5 * LOG2E)).astype(q_o.dtype)
    k = _dot(h, wk[...])
    v = _dot(h, wv[...])
    tm = k.shape[0]
    for hh in range(DA_HEADS):
        k_o[pl.ds(hh, tm, stride=DA_HEADS), :] = k[:, hh * 2 * DA_DH:(hh + 1) * 2 * DA_DH]
        v_o[pl.ds(hh, tm, stride=DA_HEADS), :] = v[:, hh * DA_VD:(hh + 1) * DA_VD]
    gq_o[...] = _dot(h, wgq[...]) * (GLA_DK ** -0.5)
    gk_o[...] = _dot(h, wgk[...])
    gv_o[...] = _dot(h, wgv[...]).astype(gv_o.dtype)
    z = _dot(_dot(h, wlr[...]), wa2[...]) + ba2[...]
    la_o[...] = _log_sigmoid(z) * (1.0 / GLA_TAU)
    og = _dot(h, wog[...])
    og_o[...] = og * jax.nn.sigmoid(og)


def _inproj_even(x, g, w_parts, w_a2, b_a2):
    m, d = x.shape
    tm = _row_tile(m, INPROJ_ROW_TILE)
    out_cols = (DA_QK, DA_QK, DA_V, GLA_QK, GLA_QK, GLA_V, GLA_QK, GLA_V)
    out_dtypes = (_MXU_DTYPE, F32, F32, F32, F32, _MXU_DTYPE, F32, F32)
    in_specs = [pl.BlockSpec((tm, d), lambda i: (i, 0)), _const_spec((1, d))]
    in_specs += [_const_spec(w.shape) for w in w_parts]
    in_specs += [_const_spec(w_a2.shape), _const_spec((1, GLA_QK))]
    return pl.pallas_call(
        _inproj_even_kernel,
        grid=(m // tm,),
        in_specs=in_specs,
        out_specs=[pl.BlockSpec((tm, n), lambda i: (i, 0)) if idx not in (1, 2)
                   else pl.BlockSpec((tm * DA_HEADS, n // DA_HEADS), lambda i: (i, 0))
                   for idx, n in enumerate(out_cols)],
        out_shape=[jax.ShapeDtypeStruct((m, n) if idx not in (1, 2) else (m * DA_HEADS, n // DA_HEADS), dt)
                   for idx, (n, dt) in enumerate(zip(out_cols, out_dtypes))],
        compiler_params=_params("parallel"),
        name="inproj_even",
    )(x, g.reshape(1, d), *w_parts, w_a2, b_a2.reshape(1, GLA_QK))


def _inproj_odd_kernel(x_ref, g_ref, wg, wx, gate_o, x_o):
    h = _mx(_rms(x_ref[...], g_ref[...]))
    gate_o[...] = _gelu_tanh(_dot(h, wg[...]))
    x_o[...] = _dot(h, wx[...])


def _inproj_odd(x, g, w_gate, w_x):
    m, d = x.shape
    tm = _row_tile(m, INPROJ_ROW_TILE)
    n = w_gate.shape[1]
    out_spec = pl.BlockSpec((tm, n), lambda i: (i, 0))
    out_shape = jax.ShapeDtypeStruct((m, n), F32)
    return pl.pallas_call(
        _inproj_odd_kernel,
        grid=(m // tm,),
        in_specs=[pl.BlockSpec((tm, d), lambda i: (i, 0)), _const_spec((1, d)),
                  _const_spec(w_gate.shape), _const_spec(w_x.shape)],
        out_specs=[out_spec] * 2,
        out_shape=[out_shape] * 2,
        compiler_params=_params("parallel"),
        name="inproj_odd",
    )(x, g.reshape(1, d), w_gate, w_x)


def _post_kernel(*refs, n_mix):
    mix_refs = refs[:n_mix]
    wo_refs = refs[n_mix:2 * n_mix]
    x_ref, g_ref, wup_ref, wdn_ref, y_ref = refs[2 * n_mix:]
    m = _dot(mix_refs[0][...], wo_refs[0][...])
    for mr, wr in zip(mix_refs[1:], wo_refs[1:]):
        m = m + _dot(mr[...], wr[...])
    g = g_ref[...]
    y1 = x_ref[...] + _rms(m, g[1:2])
    h = _mx(_rms(y1, g[2:3]))
    d_ff = wup_ref.shape[1]
    fc = min(FF_CHUNK, d_ff)
    acc = None
    for c in range(d_ff // fc):
        a = jnp.square(jnp.maximum(_dot(h, wup_ref[:, c * fc:(c + 1) * fc]), 0.0))
        part = _dot(a, wdn_ref[c * fc:(c + 1) * fc, :])
        acc = part if acc is None else acc + part
    y_ref[...] = y1 + _rms(acc, g[3:4])


def _post(mix_parts, wo_parts, x, g4, w_up, w_down):
    m, d = x.shape
    tm = _row_tile(m)
    n_mix = len(mix_parts)
    in_specs = [pl.BlockSpec((tm, p.shape[1]), lambda i: (i, 0)) for p in mix_parts]
    in_specs += [_const_spec(w.shape) for w in wo_parts]
    in_specs += [pl.BlockSpec((tm, d), lambda i: (i, 0)), _const_spec(g4.shape),
                 _const_spec(w_up.shape), _const_spec(w_down.shape)]
    return pl.pallas_call(
        functools.partial(_post_kernel, n_mix=n_mix),
        grid=(m // tm,),
        in_specs=in_specs,
        out_specs=pl.BlockSpec((tm, d), lambda i: (i, 0)),
        out_shape=jax.ShapeDtypeStruct((m, d), F32),
        compiler_params=_params("parallel"),
        name="post_mlp",
    )(*mix_parts, *wo_parts, x, g4, w_up, w_down)


def _t5_bucket(rel):
    rel = jnp.maximum(rel, 0)
    max_exact = NUM_BUCKETS // 2
    large = max_exact + (jnp.log(jnp.maximum(rel, 1).astype(F32) / max_exact)
                         / math.log(MAX_DISTANCE / max_exact) * (NUM_BUCKETS - max_exact)).astype(jnp.int32)
    large = jnp.minimum(large, NUM_BUCKETS - 1)
    return jnp.where(rel < max_exact, rel, large)


def _rel_bias(table, qpos, kpos):
    bucket = _t5_bucket(qpos[:, None] - kpos[None, :])
    return jnp.transpose(table[bucket], (2, 0, 1)).astype(F32)


def _check_far_bucket_constant(first, last):
    rel = np.arange(first, last + 1, dtype=np.float64)
    max_exact = NUM_BUCKETS // 2
    steps = np.log(rel / max_exact) / math.log(MAX_DISTANCE / max_exact) * (NUM_BUCKETS - max_exact)
    assert first >= max_exact and np.all(steps >= NUM_BUCKETS - max_exact - 0.5), "far keys must share one bias bucket"


def _lambda_from(lam_ref, lam_init):
    lv = lam_ref[...]
    s1 = jnp.sum(lv[0:1] * lv[1:2], axis=-1, keepdims=True)
    s2 = jnp.sum(lv[2:3] * lv[3:4], axis=-1, keepdims=True)
    return jnp.exp(s1) - jnp.exp(s2) + lam_init


def _attn_prompt_kernel(lam_ref, q_ref, k_ref, v_ref, bias_ref, g_ref, o_ref,
                        kb, vt, m_s, l_s, acc_s, *, tq, tk, nh, lam_init):
    i = pl.program_id(2)
    n_blk = vt.shape[0]
    hd = 2 * DA_DH
    per_q = tq // tk

    @pl.when(i == 0)
    def _():
        def prep(jj, carry):
            for h in range(nh):
                src = pl.ds(jj * (tk * nh) + h, tk, stride=nh)
                kb[pl.ds(pl.multiple_of(jj * tk, tk), tk), h * hd:(h + 1) * hd] = _mx(k_ref[0, src, :])
                vt[jj, h] = _mx(v_ref[0, src, :].T)
            return carry

        lax.fori_loop(0, n_blk, prep, 0)

    sub = lax.broadcasted_iota(jnp.int32, (hd, tq), 0)
    zero = jnp.zeros((hd, tq), F32)
    q_t = []
    for h in range(nh):
        qh = q_ref[0, :, h * hd:(h + 1) * hd].astype(F32).T
        q_t.append((_mx(jnp.where(sub < DA_DH, qh, zero)), _mx(jnp.where(sub >= DA_DH, qh, zero))))

    m_s[...] = jnp.full(m_s.shape, -jnp.inf, F32)
    l_s[...] = jnp.zeros(l_s.shape, F32)
    acc_s[...] = jnp.zeros(acc_s.shape, F32)

    def step(j, tile, q_lo=0):
        rows = pl.ds(pl.multiple_of(j * tk, tk), tk)
        qs = slice(q_lo, tq)
        for h in range(nh):
            k_blk = kb[rows, h * hd:(h + 1) * hd]
            v_blk = vt[j, h]
            for c in range(2):
                s = _dot(k_blk, q_t[h][c][:, qs])
                if tile is not None:
                    s = s + bias_ref[h, tile, :, qs]
                m_old = m_s[h, c, :, qs]
                m_new = jnp.maximum(m_old, jnp.max(s, axis=0, keepdims=True))
                alpha = jnp.exp2(m_old - m_new)
                p = jnp.exp2(s - m_new)
                l_s[h, c, :, qs] = alpha * l_s[h, c, :, qs] + jnp.sum(p, axis=0, keepdims=True)
                acc_s[h, c, :, qs] = alpha * acc_s[h, c, :, qs] + _dot(v_blk, p)
                m_s[h, c, :, qs] = m_new

    def far(jq, carry):
        for d in range(per_q):
            step(jq * per_q + d, None)
        return carry

    lax.fori_loop(0, jnp.maximum(i - 1, 0), far, 0)

    @pl.when(i > 0)
    def _():
        for d in range(per_q - 1):
            step((i - 1) * per_q + d, None)
        step(i * per_q - 1, 0)

    for d in range(per_q):
        step(i * per_q + d, 1 + d, q_lo=d * tk)

    lam = _lambda_from(lam_ref, lam_init)
    for h in range(nh):
        o = (acc_s[h, 0] / l_s[h, 0] - lam * (acc_s[h, 1] / l_s[h, 1])).T
        o_ref[0, :, h * DA_VD:(h + 1) * DA_VD] = (_rms(o, g_ref[...]) * (1.0 - lam_init)).astype(o_ref.dtype)


def _attn_prompt(q, k, v, lam_vecs, bias, g_da, lam_init):
    b, t_len, _ = q.shape
    tq, tk = _attn_tiles(t_len)
    nh = ATTN_HEADS_PER_STEP
    assert nh == DA_HEADS, "k / v arrive as (token, head) rows holding every head"
    hd = 2 * DA_DH
    return pl.pallas_call(
        functools.partial(_attn_prompt_kernel, tq=tq, tk=tk, nh=nh, lam_init=lam_init),
        grid=(b, DA_HEADS // nh, t_len // tq),
        in_specs=[
            _const_spec(lam_vecs.shape),
            pl.BlockSpec((1, tq, nh * hd), lambda bi, h, i: (bi, i, h)),
            pl.BlockSpec((1, t_len * nh, hd), lambda bi, h, i: (bi, 0, 0)),
            pl.BlockSpec((1, t_len * nh, DA_VD), lambda bi, h, i: (bi, 0, 0)),
            pl.BlockSpec((nh,) + bias.shape[1:], lambda bi, h, i: (h, 0, 0, 0)),
            _const_spec((1, DA_VD)),
        ],
        out_specs=pl.BlockSpec((1, tq, nh * DA_VD), lambda bi, h, i: (bi, i, h)),
        out_shape=jax.ShapeDtypeStruct((b, t_len, DA_V), _MXU_DTYPE),
        scratch_shapes=[
            pltpu.VMEM((t_len, nh * hd), _MXU_DTYPE),
            pltpu.VMEM((t_len // tk, nh, DA_VD, tk), _MXU_DTYPE),
            pltpu.VMEM((nh, 2, 1, tq), F32),
            pltpu.VMEM((nh, 2, 1, tq), F32),
            pltpu.VMEM((nh, 2, DA_VD, tq), F32),
        ],
        compiler_params=_params("parallel", "parallel", "arbitrary"),
        name="attn_prompt",
    )(lam_vecs, q, k, v, bias, g_da.reshape(1, DA_VD))


def _attn_tiles(t_len):
    tq = min(ATTN_Q_TILE, t_len)
    tk = min(ATTN_K_TILE, tq)
    assert t_len % tq == 0 and tq % tk == 0
    return tq, tk


def _prompt_bias(rel_table, t_len):
    tq, tk = _attn_tiles(t_len)
    if t_len > tq + tk:
        _check_far_bucket_constant(tk + 1, t_len)
    span = tk + tq
    period = span + tq
    idx = jnp.arange(period)
    dist = jnp.where(idx < span, tk - idx, tk + (period - idx))
    vec = rel_table[_t5_bucket(dist)].astype(F32) - rel_table[NUM_BUCKETS - 1].astype(F32)[None, :]
    vec = jnp.where((dist >= 0)[:, None], vec * LOG2E, NEG).T
    flat = jnp.tile(vec, (1, tq))[:, :tq * (period - 1)]
    bias = flat.reshape(vec.shape[0], tq, period - 1)[:, :, :span]
    return jnp.swapaxes(bias, 1, 2).reshape(bias.shape[0], 1 + tq // tk, tk, tq)


def _attn_sample_kernel(pt_ref, lam_ref, q_ref, kn_ref, vn_ref, bias_ref, biasn_ref, g_ref, *rest,
                        n_pg, lam_init):
    del pt_ref
    k_refs = rest[:n_pg]
    v_refs = rest[n_pg:2 * n_pg]
    o_ref, m_s, l_s, acc_s = rest[2 * n_pg:]
    step_i = pl.program_id(1)
    n_steps = pl.num_programs(1)
    rows8 = 2 * DA_HEADS
    hd = 2 * DA_DH
    pg_cols = PAGE_SIZE * DA_HEADS // 2

    @pl.when(step_i == 0)
    def _():
        m_s[...] = jnp.full(m_s.shape, -jnp.inf, F32)
        l_s[...] = jnp.zeros(l_s.shape, F32)
        acc_s[...] = jnp.zeros(acc_s.shape, F32)

    def rep2(x):
        return jnp.concatenate([x[r // 2:r // 2 + 1] for r in range(rows8)], axis=0)

    def side_by_side(x):
        even = (lax.broadcasted_iota(jnp.int32, x.shape, 0) // 2) % 2 == 0
        return jnp.concatenate([jnp.where(even, x, 0.0), jnp.where(even, 0.0, x)], axis=1)

    def page_pairs(ref):
        return jnp.concatenate([ref[0, pl.ds(0, pg_cols, stride=2), :], ref[0, pl.ds(1, pg_cols, stride=2), :]], axis=1)

    rowi = lax.broadcasted_iota(jnp.int32, (rows8, hd), 0)
    lanei = lax.broadcasted_iota(jnp.int32, (rows8, hd), 1)
    q8 = side_by_side(jnp.where(lanei // DA_DH == rowi % 2, rep2(q_ref[0].astype(F32)), 0.0))

    s = jnp.concatenate([_dot_nt(q8, page_pairs(kr)) for kr in k_refs], axis=1) + bias_ref[...]
    m_old = m_s[...]
    m_new = jnp.maximum(m_old, jnp.max(s, axis=-1, keepdims=True))
    alpha = jnp.exp2(m_old - m_new)
    p = jnp.exp2(s - m_new)
    l_new = alpha * l_s[...] + jnp.sum(p, axis=-1, keepdims=True)
    pv = _dot(p[:, 0:pg_cols], page_pairs(v_refs[0]))
    for gi in range(1, n_pg):
        pv = pv + _dot(p[:, gi * pg_cols:(gi + 1) * pg_cols], page_pairs(v_refs[gi]))
    acc_new = alpha * acc_s[...] + pv
    m_s[...] = m_new
    l_s[...] = l_new
    acc_s[...] = acc_new

    @pl.when(step_i == n_steps - 1)
    def _():
        kn = rep2(kn_ref[0])
        vn = rep2(vn_ref[0])
        sn = jnp.sum(q8 * jnp.concatenate([kn, kn], axis=1), axis=-1, keepdims=True) + biasn_ref[...]
        m_fin = jnp.maximum(m_new, sn)
        a2 = jnp.exp2(m_new - m_fin)
        pn = jnp.exp2(sn - m_fin)
        l_fin = a2 * l_new + pn
        acc_fin = a2 * acc_new + pn * jnp.concatenate([vn, vn], axis=1)
        an = acc_fin / l_fin
        lam = _lambda_from(lam_ref, lam_init)
        own = [slice((h % 2) * DA_VD, (h % 2 + 1) * DA_VD) for h in range(DA_HEADS)]
        o = jnp.concatenate([an[2 * h:2 * h + 1, own[h]] - lam * an[2 * h + 1:2 * h + 2, own[h]]
                             for h in range(DA_HEADS)], axis=0)
        o_ref[0] = (_rms(o, g_ref[...]) * (1.0 - lam_init)).astype(o_ref.dtype)


def _attn_sample(q, k_new, v_new, cache_k, cache_v, layer_idx, page_table, lam_vecs, rel_table, g_da, lam_init):
    bd = q.shape[0]
    n_pages = page_table.shape[1]
    past = n_pages * PAGE_SIZE
    n_pool = cache_k.shape[1]
    hd = 2 * DA_DH
    pg_rows = PAGE_SIZE * DA_HEADS
    n_pg = PAGES_PER_STEP if n_pages % PAGES_PER_STEP == 0 else 1
    n_steps = n_pages // n_pg
    ck = cache_k.reshape(cache_k.shape[0] * n_pool, pg_rows, hd)
    cv = cache_v.reshape(cache_v.shape[0] * n_pool, pg_rows, DA_VD)
    base = layer_idx * n_pool

    qpos = jnp.full((1,), past, jnp.int32)
    n_near = min(past, MAX_DISTANCE)
    bias_p = _rel_bias(rel_table, qpos, jnp.arange(past - n_near, past))[:, 0, :]
    if past > n_near:
        _check_far_bucket_constant(n_near + 1, past)
        far = jnp.broadcast_to(rel_table[NUM_BUCKETS - 1].astype(F32)[:, None], (DA_HEADS, past - n_near))
        bias_p = jnp.concatenate([far, bias_p], axis=1)
    n_pair = DA_HEADS // 2
    col_pair = jnp.arange(past * n_pair) % n_pair
    own_pair = col_pair[None, :] == (jnp.arange(DA_HEADS) // 2)[:, None]
    bias_p = jnp.where(own_pair, jnp.repeat(bias_p * LOG2E, n_pair, axis=1), NEG)
    bias_p = jnp.repeat(bias_p, 2, axis=0)
    bias_n = jnp.repeat(_rel_bias(rel_table, qpos, qpos)[:, 0, :] * LOG2E, 2, axis=0)

    def page_spec(gi):
        return pl.BlockSpec((1, pg_rows, hd), lambda b, s, pt: (base + pt[b, s * n_pg + gi], 0, 0))

    row3 = lambda b, s, pt: (b, 0, 0)
    const2 = lambda b, s, pt: (0, 0)
    in_specs = [
        pl.BlockSpec(lam_vecs.shape, const2),
        pl.BlockSpec((1, DA_HEADS, hd), row3),
        pl.BlockSpec((1, DA_HEADS, hd), row3),
        pl.BlockSpec((1, DA_HEADS, DA_VD), row3),
        pl.BlockSpec((2 * DA_HEADS, n_pg * pg_rows // 2), lambda b, s, pt: (0, s)),
        pl.BlockSpec((2 * DA_HEADS, 1), const2),
        pl.BlockSpec((1, DA_VD), const2),
    ]
    in_specs += [page_spec(gi) for gi in range(n_pg)] + [page_spec(gi) for gi in range(n_pg)]
    out = pl.pallas_call(
        functools.partial(_attn_sample_kernel, n_pg=n_pg, lam_init=lam_init),
        grid_spec=pltpu.PrefetchScalarGridSpec(
            num_scalar_prefetch=1,
            grid=(bd, n_steps),
            in_specs=in_specs,
            out_specs=pl.BlockSpec((1, DA_HEADS, DA_VD), row3),
            scratch_shapes=[
                pltpu.VMEM((2 * DA_HEADS, 1), F32),
                pltpu.VMEM((2 * DA_HEADS, 1), F32),
                pltpu.VMEM((2 * DA_HEADS, 2 * DA_VD), F32),
            ],
        ),
        out_shape=jax.ShapeDtypeStruct((bd, DA_HEADS, DA_VD), _MXU_DTYPE),
        compiler_params=_params("parallel", "arbitrary"),
        name="attn_sample",
    )(page_table, lam_vecs, q.reshape(bd, DA_HEADS, hd), k_new.reshape(bd, DA_HEADS, hd),
      v_new.reshape(bd, DA_HEADS, DA_VD), bias_p, bias_n, g_da.reshape(1, DA_VD), *([ck] * n_pg), *([cv] * n_pg))
    return out.reshape(bd, DA_V)


def _gla_prompt_kernel(q_ref, k_ref, v_ref, la_ref, og_ref, g_ref, o_ref, s_ref, st_s, *, chunk):
    t_len = q_ref.shape[1]
    c_sz = chunk
    pair_w = 2 * GLA_DK
    st_s[...] = jnp.zeros(st_s.shape, F32)
    r_i = lax.broadcasted_iota(jnp.int32, (c_sz, c_sz), 0)
    c_i = lax.broadcasted_iota(jnp.int32, (c_sz, c_sz), 1)
    tril = r_i >= c_i
    tril_f = tril.astype(F32)
    tril2 = jnp.concatenate([tril, tril], axis=0)
    lane = lax.broadcasted_iota(jnp.int32, (c_sz, pair_w), 1)
    first = lane < GLA_DK
    g = g_ref[...]

    def split_heads(x):
        zero = jnp.zeros_like(x)
        return jnp.concatenate([jnp.where(first, x, zero), jnp.where(first, zero, x)], axis=0)

    def body(c, carry):
        rows = pl.ds(pl.multiple_of(c * c_sz, c_sz), c_sz)
        la = la_ref[0, rows, :]
        bc = jnp.dot(tril_f, la, precision=lax.Precision.HIGHEST, preferred_element_type=F32)
        b_last = bc[c_sz - 1:c_sz, :]
        q = q_ref[0, rows, :]
        k = k_ref[0, rows, :]
        qe = q * jnp.exp(bc)
        ke = k * jnp.exp(-bc)
        kd = k * jnp.exp(b_last - bc)
        dec = jnp.exp(b_last)
        v = v_ref[0, rows, :]
        og = og_ref[0, rows, :]
        for p in range(GLA_HEADS // 2):
            ps = slice(p * pair_w, (p + 1) * pair_w)
            qs = _mx(split_heads(qe[:, ps]))
            att = jnp.where(tril2, _dot_nt(qs, ke[:, ps]), 0.0)
            st = st_s[p]
            inter = _dot_nt(qs, st)
            va = v[:, (2 * p) * GLA_DV:(2 * p + 1) * GLA_DV]
            vb = v[:, (2 * p + 1) * GLA_DV:(2 * p + 2) * GLA_DV]
            o_a = _dot(att[0:c_sz], va) + inter[0:c_sz]
            o_b = _dot(att[c_sz:2 * c_sz], vb) + inter[c_sz:2 * c_sz]
            upd = _dot_tn(jnp.concatenate([va, vb], axis=0), split_heads(kd[:, ps]))
            st_s[p] = dec[:, ps] * st + upd
            for hh, o_h in ((2 * p, o_a), (2 * p + 1, o_b)):
                cols = slice(hh * GLA_DV, (hh + 1) * GLA_DV)
                o_ref[0, rows, cols] = (_rms(o_h, g) * og[:, cols]).astype(o_ref.dtype)
        return carry

    lax.fori_loop(0, t_len // c_sz, body, 0, unroll=GLA_UNROLL)
    for p in range(GLA_HEADS // 2):
        s_t = st_s[p].T
        s_ref[0, 2 * p] = s_t[0:GLA_DK]
        s_ref[0, 2 * p + 1] = s_t[GLA_DK:2 * GLA_DK]


def _gla_prompt(gq, gk, gv, la, og, g_gla):
    b, t_len, _ = gq.shape
    chunk = GLA_CHUNK if t_len % GLA_CHUNK == 0 else t_len
    seq = lambda n: pl.BlockSpec((1, t_len, n), lambda bi: (bi, 0, 0))
    return pl.pallas_call(
        functools.partial(_gla_prompt_kernel, chunk=chunk),
        grid=(b,),
        in_specs=[seq(GLA_QK), seq(GLA_QK), seq(GLA_V), seq(GLA_QK), seq(GLA_V), _const_spec((1, GLA_DV))],
        out_specs=[seq(GLA_V), pl.BlockSpec((1, GLA_HEADS, GLA_DK, GLA_DV), lambda bi: (bi, 0, 0, 0))],
        out_shape=[jax.ShapeDtypeStruct((b, t_len, GLA_V), _MXU_DTYPE),
                   jax.ShapeDtypeStruct((b, GLA_HEADS, GLA_DK, GLA_DV), F32)],
        scratch_shapes=[pltpu.VMEM((GLA_HEADS // 2, GLA_DV, 2 * GLA_DK), F32)],
        compiler_params=_params("parallel"),
        name="gla_prompt",
    )(gq, gk, gv, la, og, g_gla.reshape(1, GLA_DV))


def _gla_sample_kernel(qc_ref, kc_ref, lac_ref, v_ref, og_ref, s0_ref, g_ref, o_ref, s_ref):
    g = g_ref[...]
    for bi in range(qc_ref.shape[0]):
        for h in range(GLA_HEADS):
            bc = lac_ref[bi, h]
            qe = qc_ref[bi, h] * jnp.exp(bc)
            ke = kc_ref[bi, h] * jnp.exp(-bc)
            kd = kc_ref[bi, h] * jnp.exp(bc - bc)
            v = v_ref[bi, h].astype(F32)
            s0 = s0_ref[bi, h]
            att = jnp.sum(qe * ke, axis=0, keepdims=True)
            o = att * v + jnp.sum(qe * s0, axis=0, keepdims=True)
            s_ref[bi, h] = jnp.exp(bc) * s0 + kd * v
            o_ref[bi, h] = (_rms(o, g) * og_ref[bi, h]).astype(o_ref.dtype)


def _gla_sample(gq, gk, gv, la, og, s0, g_gla):
    bd = gq.shape[0]
    nb = GLA_SAMPLE_SEQS_PER_STEP if bd % GLA_SAMPLE_SEQS_PER_STEP == 0 else 1
    col = lambda x: x.reshape(bd, GLA_HEADS, GLA_DK, 1)
    row = lambda x: x.reshape(bd, GLA_HEADS, 1, GLA_DV)
    col_spec = pl.BlockSpec((nb, GLA_HEADS, GLA_DK, 1), lambda b: (b, 0, 0, 0))
    row_spec = pl.BlockSpec((nb, GLA_HEADS, 1, GLA_DV), lambda b: (b, 0, 0, 0))
    st_spec = pl.BlockSpec((nb, GLA_HEADS, GLA_DK, GLA_DV), lambda b: (b, 0, 0, 0))
    o, s_new = pl.pallas_call(
        _gla_sample_kernel,
        grid=(bd // nb,),
        in_specs=[col_spec, col_spec, col_spec, row_spec, row_spec, st_spec, _const_spec((1, GLA_DV))],
        out_specs=[row_spec, st_spec],
        out_shape=[jax.ShapeDtypeStruct((bd, GLA_HEADS, 1, GLA_DV), _MXU_DTYPE),
                   jax.ShapeDtypeStruct((bd, GLA_HEADS, GLA_DK, GLA_DV), F32)],
        compiler_params=_params("parallel"),
        name="gla_sample",
    )(col(gq), col(gk), col(la), row(gv), row(og), s0, g_gla.reshape(1, GLA_DV))
    return o.reshape(bd, GLA_V), s_new


def _rglru_coeffs(xc, wga_ref, bga, wgx_ref, bgx, lam):
    bw = wga_ref.shape[1]
    r_parts, i_parts = [], []
    for n in range(wga_ref.shape[0]):
        xb = _mx(xc[:, n * bw:(n + 1) * bw])
        r_parts.append(_dot(xb, wga_ref[n]))
        i_parts.append(_dot(xb, wgx_ref[n]))
    r = _sigmoid_tanh(jnp.concatenate(r_parts, axis=1) + bga)
    i = _sigmoid_tanh(jnp.concatenate(i_parts, axis=1) + bgx)
    log_a = (-RG_C * _softplus(-lam)) * r
    a = jnp.exp(log_a)
    b = jnp.sqrt(-jnp.tanh(log_a) * (a * a + 1.0)) * (i * xc)
    return a, b


def _rglru_prompt_kernel(x_ref, gate_ref, cw_ref, cb_ref, wga_ref, bga_ref, wgx_ref, bgx_ref, lam_ref,
                         y_ref, hl_ref, xs_s, h_s, a_s, b_s):
    nb, tc, d = x_ref.shape
    pad = xs_s.shape[1] - tc

    @pl.when(pl.program_id(1) == 0)
    def _():
        xs_s[:, 0:pad, :] = jnp.zeros((nb, pad, d), F32)
        h_s[...] = jnp.zeros(h_s.shape, F32)

    cw = cw_ref[...]
    for s in range(nb):
        x = x_ref[s]
        xs_s[s, pad:pad + tc, :] = x
        xc = cb_ref[...] + cw[CONV_W - 1:CONV_W] * x
        for j in range(CONV_W - 1):
            off = pad - (CONV_W - 1) + j
            xc = xc + cw[j:j + 1] * xs_s[s, off:off + tc, :]
        xs_s[s, 0:pad, :] = x[tc - pad:tc]
        a, b = _rglru_coeffs(xc, wga_ref, bga_ref[...], wgx_ref, bgx_ref[...], lam_ref[...])
        a_s[s] = a
        b_s[s] = b

    def body(t, hs):
        row = pl.ds(t, 1)
        new = []
        for s in range(nb):
            h = a_s[s, row, :] * hs[s] + b_s[s, row, :]
            b_s[s, row, :] = h
            new.append(h)
        return tuple(new)

    hs = lax.fori_loop(0, tc, body, tuple(h_s[s] for s in range(nb)), unroll=8)
    for s in range(nb):
        h_s[s] = hs[s]
        hl_ref[s] = hs[s]
        y_ref[s] = (gate_ref[s] * b_s[s]).astype(y_ref.dtype)


def _rglru_prompt(x_br, gate, conv_w, conv_b, w_ga, b_ga, w_gx, b_gx, lam):
    b, t_len, d = x_br.shape
    tc = RNN_TIME_TILE if t_len % RNN_TIME_TILE == 0 else t_len
    nb = RNN_SEQS_PER_STEP if b % RNN_SEQS_PER_STEP == 0 else 1
    assert tc >= V7X_SUBLANES
    vec = lambda x: x.reshape(1, d)
    seq = pl.BlockSpec((nb, tc, d), lambda bi, ti: (bi, ti, 0))
    y, h_last = pl.pallas_call(
        _rglru_prompt_kernel,
        grid=(b // nb, t_len // tc),
        in_specs=[seq, seq, _const_spec(conv_w.shape), _const_spec((1, d)),
                  _const_spec(w_ga.shape), _const_spec((1, d)), _const_spec(w_gx.shape), _const_spec((1, d)),
                  _const_spec((1, d))],
        out_specs=[seq, pl.BlockSpec((nb, 1, d), lambda bi, ti: (bi, 0, 0))],
        out_shape=[jax.ShapeDtypeStruct((b, t_len, d), _MXU_DTYPE), jax.ShapeDtypeStruct((b, 1, d), F32)],
        scratch_shapes=[pltpu.VMEM((nb, V7X_SUBLANES + tc, d), F32), pltpu.VMEM((nb, 1, d), F32),
                        pltpu.VMEM((nb, tc, d), F32), pltpu.VMEM((nb, tc, d), F32)],
        compiler_params=_params("parallel", "arbitrary"),
        name="rglru_prompt",
    )(x_br, gate, conv_w, vec(conv_b), w_ga, vec(b_ga), w_gx, vec(b_gx), vec(lam))
    return y, h_last.reshape(b, d)


def _rglru_sample_kernel(x_ref, gate_ref, c0_ref, c1_ref, c2_ref, h0_ref, cw_ref, cb_ref,
                         wga_ref, bga_ref, wgx_ref, bgx_ref, lam_ref, y_ref, h_ref):
    cw = cw_ref[...]
    xc = cb_ref[...] + cw[0:1] * c0_ref[...] + cw[1:2] * c1_ref[...] + cw[2:3] * c2_ref[...] + cw[3:4] * x_ref[...]
    a, b = _rglru_coeffs(xc, wga_ref, bga_ref[...], wgx_ref, bgx_ref[...], lam_ref[...])
    h = a * h0_ref[...] + b
    h_ref[...] = h
    y_ref[...] = (gate_ref[...] * h).astype(y_ref.dtype)


def _rglru_sample(x_br, gate, conv_state, h0, conv_w, conv_b, w_ga, b_ga, w_gx, b_gx, lam):
    bd, d = x_br.shape
    vec = lambda x: x.reshape(1, d)
    args = (x_br, gate, conv_state[:, 0], conv_state[:, 1], conv_state[:, 2], h0, conv_w, vec(conv_b),
            w_ga, vec(b_ga), w_gx, vec(b_gx), vec(lam))
    return pl.pallas_call(
        _rglru_sample_kernel,
        grid=(1,),
        in_specs=[_const_spec(a.shape) for a in args],
        out_specs=[pl.BlockSpec((bd, d), lambda i: (0, 0))] * 2,
        out_shape=[jax.ShapeDtypeStruct((bd, d), _MXU_DTYPE), jax.ShapeDtypeStruct((bd, d), F32)],
        compiler_params=_params("arbitrary"),
        name="rglru_sample",
    )(*args)


def kernel(x_prompt, x_sample, cache_k, cache_v, state_gla, state_conv, state_rglru, page_table, rel_table, norm_g, w_up, w_down, w_in_even, w_alpha2, b_alpha2, lam_q1, lam_k1, lam_q2, lam_k2, g_diff, g_gla, w_out_even, w_in_odd, conv_w, conv_b, w_gate_a, b_gate_a, w_gate_x, b_gate_x, rg_lambda, w_out_odd):
    bp, t_len, d = x_prompt.shape
    bd, dec_seq, _ = x_sample.shape
    assert dec_seq == 1, "the sample group decodes one token per sequence"
    depth = norm_g.shape[0]
    yp = x_prompt.reshape(bp * t_len, d)
    ys = x_sample.reshape(bd, d)
    offs = [int(o) for o in np.cumsum(SPLIT_SIZES)]
    outs = {k: [] for k in ("kp", "vp", "ks", "vs", "gp", "gs", "cp", "cs", "rp", "rs")}

    for layer in range(depth):
        g4 = norm_g[layer]
        li = layer // 2
        wu = _mx(w_up[layer])
        wd = _mx(w_down[layer])
        if layer % 2 == 0:
            lam_init = 0.8 - 0.6 * math.exp(-0.3 * layer)
            w_in = _mx(w_in_even[li])
            w_parts = [w_in[:, a:b] for a, b in zip([0] + offs[:-1], offs)]
            w_a2 = _mx(w_alpha2[li])
            lam_vecs = jnp.stack([lam_q1[li], lam_k1[li], lam_q2[li], lam_k2[li]]).astype(F32)
            w_out = _mx(w_out_even[li])
            wo_parts = [w_out[:DA_V], w_out[DA_V:]]

            q, k, v, gq, gk, gv, la, og = _inproj_even(yp, g4[0], w_parts, w_a2, b_alpha2[li])
            bias = _prompt_bias(rel_table, t_len)
            sh = lambda x: x.reshape(bp, -1, x.shape[-1])
            a_out = _attn_prompt(sh(q), sh(k), sh(v), lam_vecs, bias, g_diff[li], lam_init)
            o_gla, s_p = _gla_prompt(sh(gq), sh(gk), sh(gv), sh(la), sh(og), g_gla[li])
            yp = _post([a_out.reshape(bp * t_len, DA_V), o_gla.reshape(bp * t_len, GLA_V)], wo_parts, yp, g4, wu, wd)
            outs["kp"].append(k.reshape(bp, t_len, DA_HEADS, 2 * DA_DH))
            outs["vp"].append(v.reshape(bp, t_len, DA_HEADS, DA_VD))
            outs["gp"].append(s_p)

            q, k, v, gq, gk, gv, la, og = _inproj_even(ys, g4[0], w_parts, w_a2, b_alpha2[li])
            a_out = _attn_sample(q, k, v, cache_k, cache_v, li, page_table, lam_vecs, rel_table, g_diff[li], lam_init)
            o_gla, s_s = _gla_sample(gq, gk, gv, la, og, state_gla[li], g_gla[li])
            ys = _post([a_out, o_gla], wo_parts, ys, g4, wu, wd)
            outs["ks"].append(k.reshape(bd, 1, DA_HEADS, 2 * DA_DH))
            outs["vs"].append(v.reshape(bd, 1, DA_HEADS, DA_VD))
            outs["gs"].append(s_s)
        else:
            w_in = _mx(w_in_odd[li])
            d_rnn = w_in.shape[1] // 2
            w_gate, w_x = w_in[:, :d_rnn], w_in[:, d_rnn:]
            wga, wgx = _mx(w_gate_a[li]), _mx(w_gate_x[li])
            w_out = [_mx(w_out_odd[li])]
            rnn = (conv_w[li], conv_b[li], wga, b_gate_a[li], wgx, b_gate_x[li], rg_lambda[li])

            gate, x_br = _inproj_odd(yp, g4[0], w_gate, w_x)
            x_br3 = x_br.reshape(bp, t_len, d_rnn)
            y, h_last = _rglru_prompt(x_br3, gate.reshape(bp, t_len, d_rnn), *rnn)
            yp = _post([y.reshape(bp * t_len, d_rnn)], w_out, yp, g4, wu, wd)
            assert t_len >= CONV_W - 1
            outs["cp"].append(x_br3[:, t_len - (CONV_W - 1):])
            outs["rp"].append(h_last)

            gate, x_br = _inproj_odd(ys, g4[0], w_gate, w_x)
            y, h_new = _rglru_sample(x_br, gate, state_conv[li], state_rglru[li], *rnn)
            ys = _post([y], w_out, ys, g4, wu, wd)
            outs["cs"].append(jnp.concatenate([state_conv[li][:, 1:], x_br[:, None, :]], axis=1))
            outs["rs"].append(h_new)

    st = lambda key: jnp.stack(outs[key])
    return (yp.reshape(bp, t_len, d), ys.reshape(bd, 1, d), st("kp"), st("vp"), st("ks"), st("vs"),
            st("gp"), st("gs"), st("cp"), st("cs"), st("rp"), st("rs"))
```

```python
import functools
import math

import numpy as np
import jax
import jax.numpy as jnp
from jax import lax
from jax.experimental import pallas as pl
from jax.experimental.pallas import tpu as pltpu

DA_HEADS = 4
DA_DH = 64
DA_VD = 2 * DA_DH
GLA_HEADS = 4
GLA_DK = 64
GLA_DV = 128
GLA_RANK = 16
GLA_TAU = 16.0
DA_QK = DA_HEADS * 2 * DA_DH
DA_V = DA_HEADS * DA_VD
GLA_QK = GLA_HEADS * GLA_DK
GLA_V = GLA_HEADS * GLA_DV
SPLIT_SIZES = (DA_QK, DA_QK, DA_V, GLA_QK, GLA_QK, GLA_V, GLA_RANK, GLA_V)
RNN_BLOCKS = 8
CONV_W = 4
RG_C = 8.0
NUM_BUCKETS = 32
MAX_DISTANCE = 128
PAGE_SIZE = 128
EPS = 1e-6
NEG = -1e30
LOG2E = math.log2(math.e)

V7X_LANES = 128
V7X_SUBLANES = 8
V7X_VMEM_LIMIT_BYTES = 56 * 1024 * 1024

F32 = jnp.float32
_MXU_DTYPE = jnp.bfloat16

ROW_TILE = 512
INPROJ_ROW_TILE = 1024
FF_CHUNK = 1024
ATTN_Q_TILE = 256
ATTN_K_TILE = 128
ATTN_HEADS_PER_STEP = 4
GLA_CHUNK = 64
GLA_SAMPLE_SEQS_PER_STEP = 8
GLA_UNROLL = 4
RNN_TIME_TILE = 128
RNN_SEQS_PER_STEP = 4
PAGES_PER_STEP = 32


def _mx(x):
    return x.astype(_MXU_DTYPE)


def _dot(a, b):
    return jnp.dot(_mx(a), _mx(b), preferred_element_type=F32)


def _dot_nt(a, b):
    return lax.dot_general(_mx(a), _mx(b), (((1,), (1,)), ((), ())), preferred_element_type=F32)


def _dot_tn(a, b):
    return lax.dot_general(_mx(a), _mx(b), (((0,), (0,)), ((), ())), preferred_element_type=F32)


def _rms(x, g):
    return x * lax.rsqrt(jnp.mean(x * x, axis=-1, keepdims=True) + EPS) * g


def _softplus(x):
    return jnp.maximum(x, 0.0) + jnp.log1p(jnp.exp(-jnp.abs(x)))


def _log_sigmoid(x):
    return -_softplus(-x)


def _sigmoid_tanh(x):
    return 0.5 * jnp.tanh(0.5 * x) + 0.5


def _gelu_tanh(x):
    c = math.sqrt(2.0 / math.pi)
    return 0.5 * x * (1.0 + jnp.tanh(c * (x + 0.044715 * (x * x * x))))


def _params(*sem):
    return pltpu.CompilerParams(dimension_semantics=sem, vmem_limit_bytes=V7X_VMEM_LIMIT_BYTES)


def _const_spec(shape):
    nd = len(shape)
    return pl.BlockSpec(shape, lambda *_: (0,) * nd, pipeline_mode=pl.Buffered(1))


def _row_tile(m, tile=None):
    tile = ROW_TILE if tile is None else tile
    return tile if m % tile == 0 else m


def _inproj_even_kernel(x_ref, g_ref, wq, wk, wv, wgq, wgk, wgv, wlr, wog, wa2, ba2,
                        q_o, k_o, v_o, gq_o, gk_o, gv_o, la_o, og_o):
    h = _mx(_rms(x_ref[...], g_ref[...]))
    q_o[...] = (_dot(h, wq[...]) * (DA_DH ** -0.5 * LOG2E)).astype(q_o.dtype)
    k = _dot(h, wk[...])
    v = _dot(h, wv[...])
    tm = k.shape[0]
    for hh in range(DA_HEADS):
        k_o[pl.ds(hh, tm, stride=DA_HEADS), :] = k[:, hh * 2 * DA_DH:(hh + 1) * 2 * DA_DH]
        v_o[pl.ds(hh, tm, stride=DA_HEADS), :] = v[:, hh * DA_VD:(hh + 1) * DA_VD]
    gq_o[...] = _dot(h, wgq[...]) * (GLA_DK ** -0.5)
    gk_o[...] = _dot(h, wgk[...])
    gv_o[...] = _dot(h, wgv[...]).astype(gv_o.dtype)
    z = _dot(_dot(h, wlr[...]), wa2[...]) + ba2[...]
    la_o[...] = _log_sigmoid(z) * (1.0 / GLA_TAU)
    og = _dot(h, wog[...])
    og_o[...] = og * jax.nn.sigmoid(og)


def _inproj_even(x, g, w_parts, w_a2, b_a2):
    m, d = x.shape
    tm = _row_tile(m, INPROJ_ROW_TILE)
    out_cols = (DA_QK, DA_QK, DA_V, GLA_QK, GLA_QK, GLA_V, GLA_QK, GLA_V)
    out_dtypes = (_MXU_DTYPE, F32, F32, F32, F32, _MXU_DTYPE, F32, F32)
    in_specs = [pl.BlockSpec((tm, d), lambda i: (i, 0)), _const_spec((1, d))]
    in_specs += [_const_spec(w.shape) for w in w_parts]
    in_specs += [_const_spec(w_a2.shape), _const_spec((1, GLA_QK))]
    return pl.pallas_call(
        _inproj_even_kernel,
        grid=(m // tm,),
        in_specs=in_specs,
        out_specs=[pl.BlockSpec((tm, n), lambda i: (i, 0)) if idx not in (1, 2)
                   else pl.BlockSpec((tm * DA_HEADS, n // DA_HEADS), lambda i: (i, 0))
                   for idx, n in enumerate(out_cols)],
        out_shape=[jax.ShapeDtypeStruct((m, n) if idx not in (1, 2) else (m * DA_HEADS, n // DA_HEADS), dt)
                   for idx, (n, dt) in enumerate(zip(out_cols, out_dtypes))],
        compiler_params=_params("parallel"),
        name="inproj_even",
    )(x, g.reshape(1, d), *w_parts, w_a2, b_a2.reshape(1, GLA_QK))


def _inproj_odd_kernel(x_ref, g_ref, wg, wx, gate_o, x_o):
    h = _mx(_rms(x_ref[...], g_ref[...]))
    gate_o[...] = _gelu_tanh(_dot(h, wg[...]))
    x_o[...] = _dot(h, wx[...])


def _inproj_odd(x, g, w_gate, w_x):
    m, d = x.shape
    tm = _row_tile(m, INPROJ_ROW_TILE)
    n = w_gate.shape[1]
    out_spec = pl.BlockSpec((tm, n), lambda i: (i, 0))
    out_shape = jax.ShapeDtypeStruct((m, n), F32)
    return pl.pallas_call(
        _inproj_odd_kernel,
        grid=(m // tm,),
        in_specs=[pl.BlockSpec((tm, d), lambda i: (i, 0)), _const_spec((1, d)),
                  _const_spec(w_gate.shape), _const_spec(w_x.shape)],
        out_specs=[out_spec] * 2,
        out_shape=[out_shape] * 2,
        compiler_params=_params("parallel"),
        name="inproj_odd",
    )(x, g.reshape(1, d), w_gate, w_x)


def _post_kernel(*refs, n_mix):
    mix_refs = refs[:n_mix]
    wo_refs = refs[n_mix:2 * n_mix]
    x_ref, g_ref, wup_ref, wdn_ref, y_ref = refs[2 * n_mix:]
    m = _dot(mix_refs[0][...], wo_refs[0][...])
    for mr, wr in zip(mix_refs[1:], wo_refs[1:]):
        m = m + _dot(mr[...], wr[...])
    g = g_ref[...]
    y1 = x_ref[...] + _rms(m, g[1:2])
    h = _mx(_rms(y1, g[2:3]))
    d_ff = wup_ref.shape[1]
    fc = min(FF_CHUNK, d_ff)
    acc = None
    for c in range(d_ff // fc):
        a = jnp.square(jnp.maximum(_dot(h, wup_ref[:, c * fc:(c + 1) * fc]), 0.0))
        part = _dot(a, wdn_ref[c * fc:(c + 1) * fc, :])
        acc = part if acc is None else acc + part
    y_ref[...] = y1 + _rms(acc, g[3:4])


def _post(mix_parts, wo_parts, x, g4, w_up, w_down):
    m, d = x.shape
    tm = _row_tile(m)
    n_mix = len(mix_parts)
    in_specs = [pl.BlockSpec((tm, p.shape[1]), lambda i: (i, 0)) for p in mix_parts]
    in_specs += [_const_spec(w.shape) for w in wo_parts]
    in_specs += [pl.BlockSpec((tm, d), lambda i: (i, 0)), _const_spec(g4.shape),
                 _const_spec(w_up.shape), _const_spec(w_down.shape)]
    return pl.pallas_call(
        functools.partial(_post_kernel, n_mix=n_mix),
        grid=(m // tm,),
        in_specs=in_specs,
        out_specs=pl.BlockSpec((tm, d), lambda i: (i, 0)),
        out_shape=jax.ShapeDtypeStruct((m, d), F32),
        compiler_params=_params("parallel"),
        name="post_mlp",
    )(*mix_parts, *wo_parts, x, g4, w_up, w_down)


def _t5_bucket(rel):
    rel = jnp.maximum(rel, 0)
    max_exact = NUM_BUCKETS // 2
    large = max_exact + (jnp.log(jnp.maximum(rel, 1).astype(F32) / max_exact)
                         / math.log(MAX_DISTANCE / max_exact) * (NUM_BUCKETS - max_exact)).astype(jnp.int32)
    large = jnp.minimum(large, NUM_BUCKETS - 1)
    return jnp.where(rel < max_exact, rel, large)


def _rel_bias(table, qpos, kpos):
    bucket = _t5_bucket(qpos[:, None] - kpos[None, :])
    return jnp.transpose(table[bucket], (2, 0, 1)).astype(F32)


def _check_far_bucket_constant(first, last):
    rel = np.arange(first, last + 1, dtype=np.float64)
    max_exact = NUM_BUCKETS // 2
    steps = np.log(rel / max_exact) / math.log(MAX_DISTANCE / max_exact) * (NUM_BUCKETS - max_exact)
    assert first >= max_exact and np.all(steps >= NUM_BUCKETS - max_exact - 0.5), "far keys must share one bias bucket"


def _lambda_from(lam_ref, lam_init):
    lv = lam_ref[...]
    s1 = jnp.sum(lv[0:1] * lv[1:2], axis=-1, keepdims=True)
    s2 = jnp.sum(lv[2:3] * lv[3:4], axis=-1, keepdims=True)
    return jnp.exp(s1) - jnp.exp(s2) + lam_init


def _attn_prompt_kernel(lam_ref, q_ref, k_ref, v_ref, bias_ref, g_ref, o_ref,
                        kb, vt, m_s, l_s, acc_s, *, tq, tk, nh, lam_init):
    i = pl.program_id(2)
    n_blk = vt.shape[0]
    hd = 2 * DA_DH
    per_q = tq // tk

    @pl.when(i == 0)
    def _():
        def prep(jj, carry):
            for h in range(nh):
                src = pl.ds(jj * (tk * nh) + h, tk, stride=nh)
                kb[pl.ds(pl.multiple_of(jj * tk, tk), tk), h * hd:(h + 1) * hd] = _mx(k_ref[0, src, :])
                vt[jj, h] = _mx(v_ref[0, src, :].T)
            return carry

        lax.fori_loop(0, n_blk, prep, 0)

    sub = lax.broadcasted_iota(jnp.int32, (hd, tq), 0)
    zero = jnp.zeros((hd, tq), F32)
    q_t = []
    for h in range(nh):
        qh = q_ref[0, :, h * hd:(h + 1) * hd].astype(F32).T
        q_t.append((_mx(jnp.where(sub < DA_DH, qh, zero)), _mx(jnp.where(sub >= DA_DH, qh, zero))))

    m_s[...] = jnp.full(m_s.shape, -jnp.inf, F32)
    l_s[...] = jnp.zeros(l_s.shape, F32)
    acc_s[...] = jnp.zeros(acc_s.shape, F32)

    def step(j, tile, q_lo=0):
        rows = pl.ds(pl.multiple_of(j * tk, tk), tk)
        qs = slice(q_lo, tq)
        for h in range(nh):
            k_blk = kb[rows, h * hd:(h + 1) * hd]
            v_blk = vt[j, h]
            for c in range(2):
                s = _dot(k_blk, q_t[h][c][:, qs])
                if tile is not None:
                    s = s + bias_ref[h, tile, :, qs]
                m_old = m_s[h, c, :, qs]
                m_new = jnp.maximum(m_old, jnp.max(s, axis=0, keepdims=True))
                alpha = jnp.exp2(m_old - m_new)
                p = jnp.exp2(s - m_new)
                l_s[h, c, :, qs] = alpha * l_s[h, c, :, qs] + jnp.sum(p, axis=0, keepdims=True)
                acc_s[h, c, :, qs] = alpha * acc_s[h, c, :, qs] + _dot(v_blk, p)
                m_s[h, c, :, qs] = m_new

    def far(jq, carry):
        for d in range(per_q):
            step(jq * per_q + d, None)
        return carry

    lax.fori_loop(0, jnp.maximum(i - 1, 0), far, 0)

    @pl.when(i > 0)
    def _():
        for d in range(per_q - 1):
            step((i - 1) * per_q + d, None)
        step(i * per_q - 1, 0)

    for d in range(per_q):
        step(i * per_q + d, 1 + d, q_lo=d * tk)

    lam = _lambda_from(lam_ref, lam_init)
    for h in range(nh):
        o = (acc_s[h, 0] / l_s[h, 0] - lam * (acc_s[h, 1] / l_s[h, 1])).T
        o_ref[0, :, h * DA_VD:(h + 1) * DA_VD] = (_rms(o, g_ref[...]) * (1.0 - lam_init)).astype(o_ref.dtype)


def _attn_prompt(q, k, v, lam_vecs, bias, g_da, lam_init):
    b, t_len, _ = q.shape
    tq, tk = _attn_tiles(t_len)
    nh = ATTN_HEADS_PER_STEP
    assert nh == DA_HEADS, "k / v arrive as (token, head) rows holding every head"
    hd = 2 * DA_DH
    return pl.pallas_call(
        functools.partial(_attn_prompt_kernel, tq=tq, tk=tk, nh=nh, lam_init=lam_init),
        grid=(b, DA_HEADS // nh, t_len // tq),
        in_specs=[
            _const_spec(lam_vecs.shape),
            pl.BlockSpec((1, tq, nh * hd), lambda bi, h, i: (bi, i, h)),
            pl.BlockSpec((1, t_len * nh, hd), lambda bi, h, i: (bi, 0, 0)),
            pl.BlockSpec((1, t_len * nh, DA_VD), lambda bi, h, i: (bi, 0, 0)),
            pl.BlockSpec((nh,) + bias.shape[1:], lambda bi, h, i: (h, 0, 0, 0)),
            _const_spec((1, DA_VD)),
        ],
        out_specs=pl.BlockSpec((1, tq, nh * DA_VD), lambda bi, h, i: (bi, i, h)),
        out_shape=jax.ShapeDtypeStruct((b, t_len, DA_V), _MXU_DTYPE),
        scratch_shapes=[
            pltpu.VMEM((t_len, nh * hd), _MXU_DTYPE),
            pltpu.VMEM((t_len // tk, nh, DA_VD, tk), _MXU_DTYPE),
            pltpu.VMEM((nh, 2, 1, tq), F32),
            pltpu.VMEM((nh, 2, 1, tq), F32),
            pltpu.VMEM((nh, 2, DA_VD, tq), F32),
        ],
        compiler_params=_params("parallel", "parallel", "arbitrary"),
        name="attn_prompt",
    )(lam_vecs, q, k, v, bias, g_da.reshape(1, DA_VD))


def _attn_tiles(t_len):
    tq = min(ATTN_Q_TILE, t_len)
    tk = min(ATTN_K_TILE, tq)
    assert t_len % tq == 0 and tq % tk == 0
    return tq, tk


def _prompt_bias(rel_table, t_len):
    tq, tk = _attn_tiles(t_len)
    if t_len > tq + tk:
        _check_far_bucket_constant(tk + 1, t_len)
    span = tk + tq
    period = span + tq
    idx = jnp.arange(period)
    dist = jnp.where(idx < span, tk - idx, tk + (period - idx))
    vec = rel_table[_t5_bucket(dist)].astype(F32) - rel_table[NUM_BUCKETS - 1].astype(F32)[None, :]
    vec = jnp.where((dist >= 0)[:, None], vec * LOG2E, NEG).T
    flat = jnp.tile(vec, (1, tq))[:, :tq * (period - 1)]
    bias = flat.reshape(vec.shape[0], tq, period - 1)[:, :, :span]
    return jnp.swapaxes(bias, 1, 2).reshape(bias.shape[0], 1 + tq // tk, tk, tq)


def _attn_sample_kernel(pt_ref, lam_ref, q_ref, kn_ref, vn_ref, bias_ref, biasn_ref, g_ref, *rest,
                        n_pg, lam_init):
    del pt_ref
    k_refs = rest[:n_pg]
    v_refs = rest[n_pg:2 * n_pg]
    o_ref, m_s, l_s, acc_s = rest[2 * n_pg:]
    step_i = pl.program_id(1)
    n_steps = pl.num_programs(1)
    rows8 = 2 * DA_HEADS
    hd = 2 * DA_DH
    pg_cols = PAGE_SIZE * DA_HEADS // 2

    @pl.when(step_i == 0)
    def _():
        m_s[...] = jnp.full(m_s.shape, -jnp.inf, F32)
        l_s[...] = jnp.zeros(l_s.shape, F32)
        acc_s[...] = jnp.zeros(acc_s.shape, F32)

    def rep2(x):
        return jnp.concatenate([x[r // 2:r // 2 + 1] for r in range(rows8)], axis=0)

    def side_by_side(x):
        even = (lax.broadcasted_iota(jnp.int32, x.shape, 0) // 2) % 2 == 0
        return jnp.concatenate([jnp.where(even, x, 0.0), jnp.where(even, 0.0, x)], axis=1)

    def page_pairs(ref):
        return jnp.concatenate([ref[0, pl.ds(0, pg_cols, stride=2), :], ref[0, pl.ds(1, pg_cols, stride=2), :]], axis=1)

    rowi = lax.broadcasted_iota(jnp.int32, (rows8, hd), 0)
    lanei = lax.broadcasted_iota(jnp.int32, (rows8, hd), 1)
    q8 = side_by_side(jnp.where(lanei // DA_DH == rowi % 2, rep2(q_ref[0].astype(F32)), 0.0))

    s = jnp.concatenate([_dot_nt(q8, page_pairs(kr)) for kr in k_refs], axis=1) + bias_ref[...]
    m_old = m_s[...]
    m_new = jnp.maximum(m_old, jnp.max(s, axis=-1, keepdims=True))
    alpha = jnp.exp2(m_old - m_new)
    p = jnp.exp2(s - m_new)
    l_new = alpha * l_s[...] + jnp.sum(p, axis=-1, keepdims=True)
    pv = _dot(p[:, 0:pg_cols], page_pairs(v_refs[0]))
    for gi in range(1, n_pg):
        pv = pv + _dot(p[:, gi * pg_cols:(gi + 1) * pg_cols], page_pairs(v_refs[gi]))
    acc_new = alpha * acc_s[...] + pv
    m_s[...] = m_new
    l_s[...] = l_new
    acc_s[...] = acc_new

    @pl.when(step_i == n_steps - 1)
    def _():
        kn = rep2(kn_ref[0])
        vn = rep2(vn_ref[0])
        sn = jnp.sum(q8 * jnp.concatenate([kn, kn], axis=1), axis=-1, keepdims=True) + biasn_ref[...]
        m_fin = jnp.maximum(m_new, sn)
        a2 = jnp.exp2(m_new - m_fin)
        pn = jnp.exp2(sn - m_fin)
        l_fin = a2 * l_new + pn
        acc_fin = a2 * acc_new + pn * jnp.concatenate([vn, vn], axis=1)
        an = acc_fin / l_fin
        lam = _lambda_from(lam_ref, lam_init)
        own = [slice((h % 2) * DA_VD, (h % 2 + 1) * DA_VD) for h in range(DA_HEADS)]
        o = jnp.concatenate([an[2 * h:2 * h + 1, own[h]] - lam * an[2 * h + 1:2 * h + 2, own[h]]
                             for h in range(DA_HEADS)], axis=0)
        o_ref[0] = (_rms(o, g_ref[...]) * (1.0 - lam_init)).astype(o_ref.dtype)


def _attn_sample(q, k_new, v_new, cache_k, cache_v, layer_idx, page_table, lam_vecs, rel_table, g_da, lam_init):
    bd = q.shape[0]
    n_pages = page_table.shape[1]
    past = n_pages * PAGE_SIZE
    n_pool = cache_k.shape[1]
    hd = 2 * DA_DH
    pg_rows = PAGE_SIZE * DA_HEADS
    n_pg = PAGES_PER_STEP if n_pages % PAGES_PER_STEP == 0 else 1
    n_steps = n_pages // n_pg
    ck = cache_k.reshape(cache_k.shape[0] * n_pool, pg_rows, hd)
    cv = cache_v.reshape(cache_v.shape[0] * n_pool, pg_rows, DA_VD)
    base = layer_idx * n_pool

    qpos = jnp.full((1,), past, jnp.int32)
    n_near = min(past, MAX_DISTANCE)
    bias_p = _rel_bias(rel_table, qpos, jnp.arange(past - n_near, past))[:, 0, :]
    if past > n_near:
        _check_far_bucket_constant(n_near + 1, past)
        far = jnp.broadcast_to(rel_table[NUM_BUCKETS - 1].astype(F32)[:, None], (DA_HEADS, past - n_near))
        bias_p = jnp.concatenate([far, bias_p], axis=1)
    n_pair = DA_HEADS // 2
    col_pair = jnp.arange(past * n_pair) % n_pair
    own_pair = col_pair[None, :] == (jnp.arange(DA_HEADS) // 2)[:, None]
    bias_p = jnp.where(own_pair, jnp.repeat(bias_p * LOG2E, n_pair, axis=1), NEG)
    bias_p = jnp.repeat(bias_p, 2, axis=0)
    bias_n = jnp.repeat(_rel_bias(rel_table, qpos, qpos)[:, 0, :] * LOG2E, 2, axis=0)

    def page_spec(gi):
        return pl.BlockSpec((1, pg_rows, hd), lambda b, s, pt: (base + pt[b, s * n_pg + gi], 0, 0))

    row3 = lambda b, s, pt: (b, 0, 0)
    const2 = lambda b, s, pt: (0, 0)
    in_specs = [
        pl.BlockSpec(lam_vecs.shape, const2),
        pl.BlockSpec((1, DA_HEADS, hd), row3),
        pl.BlockSpec((1, DA_HEADS, hd), row3),
        pl.BlockSpec((1, DA_HEADS, DA_VD), row3),
        pl.BlockSpec((2 * DA_HEADS, n_pg * pg_rows // 2), lambda b, s, pt: (0, s)),
        pl.BlockSpec((2 * DA_HEADS, 1), const2),
        pl.BlockSpec((1, DA_VD), const2),
    ]
    in_specs += [page_spec(gi) for gi in range(n_pg)] + [page_spec(gi) for gi in range(n_pg)]
    out = pl.pallas_call(
        functools.partial(_attn_sample_kernel, n_pg=n_pg, lam_init=lam_init),
        grid_spec=pltpu.PrefetchScalarGridSpec(
            num_scalar_prefetch=1,
            grid=(bd, n_steps),
            in_specs=in_specs,
            out_specs=pl.BlockSpec((1, DA_HEADS, DA_VD), row3),
            scratch_shapes=[
                pltpu.VMEM((2 * DA_HEADS, 1), F32),
                pltpu.VMEM((2 * DA_HEADS, 1), F32),
                pltpu.VMEM((2 * DA_HEADS, 2 * DA_VD), F32),
            ],
        ),
        out_shape=jax.ShapeDtypeStruct((bd, DA_HEADS, DA_VD), _MXU_DTYPE),
        compiler_params=_params("parallel", "arbitrary"),
        name="attn_sample",
    )(page_table, lam_vecs, q.reshape(bd, DA_HEADS, hd), k_new.reshape(bd, DA_HEADS, hd),
      v_new.reshape(bd, DA_HEADS, DA_VD), bias_p, bias_n, g_da.reshape(1, DA_VD), *([ck] * n_pg), *([cv] * n_pg))
    return out.reshape(bd, DA_V)


def _gla_prompt_kernel(q_ref, k_ref, v_ref, la_ref, og_ref, g_ref, o_ref, s_ref, st_s, *, chunk):
    t_len = q_ref.shape[1]
    c_sz = chunk
    pair_w = 2 * GLA_DK
    st_s[...] = jnp.zeros(st_s.shape, F32)
    r_i = lax.broadcasted_iota(jnp.int32, (c_sz, c_sz), 0)
    c_i = lax.broadcasted_iota(jnp.int32, (c_sz, c_sz), 1)
    tril = r_i >= c_i
    tril_f = tril.astype(F32)
    tril2 = jnp.concatenate([tril, tril], axis=0)
    lane = lax.broadcasted_iota(jnp.int32, (c_sz, pair_w), 1)
    first = lane < GLA_DK
    g = g_ref[...]

    def split_heads(x):
        zero = jnp.zeros_like(x)
        return jnp.concatenate([jnp.where(first, x, zero), jnp.where(first, zero, x)], axis=0)

    def body(c, carry):
        rows = pl.ds(pl.multiple_of(c * c_sz, c_sz), c_sz)
        la = la_ref[0, rows, :]
        bc = jnp.dot(tril_f, la, precision=lax.Precision.HIGHEST, preferred_element_type=F32)
        b_last = bc[c_sz - 1:c_sz, :]
        q = q_ref[0, rows, :]
        k = k_ref[0, rows, :]
        qe = q * jnp.exp(bc)
        ke = k * jnp.exp(-bc)
        kd = k * jnp.exp(b_last - bc)
        dec = jnp.exp(b_last)
        v = v_ref[0, rows, :]
        og = og_ref[0, rows, :]
        for p in range(GLA_HEADS // 2):
            ps = slice(p * pair_w, (p + 1) * pair_w)
            qs = _mx(split_heads(qe[:, ps]))
            att = jnp.where(tril2, _dot_nt(qs, ke[:, ps]), 0.0)
            st = st_s[p]
            inter = _dot_nt(qs, st)
            va = v[:, (2 * p) * GLA_DV:(2 * p + 1) * GLA_DV]
            vb = v[:, (2 * p + 1) * GLA_DV:(2 * p + 2) * GLA_DV]
            o_a = _dot(att[0:c_sz], va) + inter[0:c_sz]
            o_b = _dot(att[c_sz:2 * c_sz], vb) + inter[c_sz:2 * c_sz]
            upd = _dot_tn(jnp.concatenate([va, vb], axis=0), split_heads(kd[:, ps]))
            st_s[p] = dec[:, ps] * st + upd
            for hh, o_h in ((2 * p, o_a), (2 * p + 1, o_b)):
                cols = slice(hh * GLA_DV, (hh + 1) * GLA_DV)
                o_ref[0, rows, cols] = (_rms(o_h, g) * og[:, cols]).astype(o_ref.dtype)
        return carry

    lax.fori_loop(0, t_len // c_sz, body, 0, unroll=GLA_UNROLL)
    for p in range(GLA_HEADS // 2):
        s_t = st_s[p].T
        s_ref[0, 2 * p] = s_t[0:GLA_DK]
        s_ref[0, 2 * p + 1] = s_t[GLA_DK:2 * GLA_DK]


def _gla_prompt(gq, gk, gv, la, og, g_gla):
    b, t_len, _ = gq.shape
    chunk = GLA_CHUNK if t_len % GLA_CHUNK == 0 else t_len
    seq = lambda n: pl.BlockSpec((1, t_len, n), lambda bi: (bi, 0, 0))
    return pl.pallas_call(
        functools.partial(_gla_prompt_kernel, chunk=chunk),
        grid=(b,),
        in_specs=[seq(GLA_QK), seq(GLA_QK), seq(GLA_V), seq(GLA_QK), seq(GLA_V), _const_spec((1, GLA_DV))],
        out_specs=[seq(GLA_V), pl.BlockSpec((1, GLA_HEADS, GLA_DK, GLA_DV), lambda bi: (bi, 0, 0, 0))],
        out_shape=[jax.ShapeDtypeStruct((b, t_len, GLA_V), _MXU_DTYPE),
                   jax.ShapeDtypeStruct((b, GLA_HEADS, GLA_DK, GLA_DV), F32)],
        scratch_shapes=[pltpu.VMEM((GLA_HEADS // 2, GLA_DV, 2 * GLA_DK), F32)],
        compiler_params=_params("parallel"),
        name="gla_prompt",
    )(gq, gk, gv, la, og, g_gla.reshape(1, GLA_DV))


def _gla_sample_kernel(qc_ref, kc_ref, lac_ref, v_ref, og_ref, s0_ref, g_ref, o_ref, s_ref):
    g = g_ref[...]
    for bi in range(qc_ref.shape[0]):
        for h in range(GLA_HEADS):
            bc = lac_ref[bi, h]
            qe = qc_ref[bi, h] * jnp.exp(bc)
            ke = kc_ref[bi, h] * jnp.exp(-bc)
            kd = kc_ref[bi, h] * jnp.exp(bc - bc)
            v = v_ref[bi, h].astype(F32)
            s0 = s0_ref[bi, h]
            att = jnp.sum(qe * ke, axis=0, keepdims=True)
            o = att * v + jnp.sum(qe * s0, axis=0, keepdims=True)
            s_ref[bi, h] = jnp.exp(bc) * s0 + kd * v
            o_ref[bi, h] = (_rms(o, g) * og_ref[bi, h]).astype(o_ref.dtype)


def _gla_sample(gq, gk, gv, la, og, s0, g_gla):
    bd = gq.shape[0]
    nb = GLA_SAMPLE_SEQS_PER_STEP if bd % GLA_SAMPLE_SEQS_PER_STEP == 0 else 1
    col = lambda x: x.reshape(bd, GLA_HEADS, GLA_DK, 1)
    row = lambda x: x.reshape(bd, GLA_HEADS, 1, GLA_DV)
    col_spec = pl.BlockSpec((nb, GLA_HEADS, GLA_DK, 1), lambda b: (b, 0, 0, 0))
    row_spec = pl.BlockSpec((nb, GLA_HEADS, 1, GLA_DV), lambda b: (b, 0, 0, 0))
    st_spec = pl.BlockSpec((nb, GLA_HEADS, GLA_DK, GLA_DV), lambda b: (b, 0, 0, 0))
    o, s_new = pl.pallas_call(
        _gla_sample_kernel,
        grid=(bd // nb,),
        in_specs=[col_spec, col_spec, col_spec, row_spec, row_spec, st_spec, _const_spec((1, GLA_DV))],
        out_specs=[row_spec, st_spec],
        out_shape=[jax.ShapeDtypeStruct((bd, GLA_HEADS, 1, GLA_DV), _MXU_DTYPE),
                   jax.ShapeDtypeStruct((bd, GLA_HEADS, GLA_DK, GLA_DV), F32)],
        compiler_params=_params("parallel"),
        name="gla_sample",
    )(col(gq), col(gk), col(la), row(gv), row(og), s0, g_gla.reshape(1, GLA_DV))
    return o.reshape(bd, GLA_V), s_new


def _rglru_coeffs(xc, wga_ref, bga, wgx_ref, bgx, lam):
    bw = wga_ref.shape[1]
    r_parts, i_parts = [], []
    for n in range(wga_ref.shape[0]):
        xb = _mx(xc[:, n * bw:(n + 1) * bw])
        r_parts.append(_dot(xb, wga_ref[n]))
        i_parts.append(_dot(xb, wgx_ref[n]))
    r = _sigmoid_tanh(jnp.concatenate(r_parts, axis=1) + bga)
    i = _sigmoid_tanh(jnp.concatenate(i_parts, axis=1) + bgx)
    log_a = (-RG_C * _softplus(-lam)) * r
    a = jnp.exp(log_a)
    b = jnp.sqrt(-jnp.tanh(log_a) * (a * a + 1.0)) * (i * xc)
    return a, b


def _rglru_prompt_kernel(x_ref, gate_ref, cw_ref, cb_ref, wga_ref, bga_ref, wgx_ref, bgx_ref, lam_ref,
                         y_ref, hl_ref, xs_s, h_s, a_s, b_s):
    nb, tc, d = x_ref.shape
    pad = xs_s.shape[1] - tc

    @pl.when(pl.program_id(1) == 0)
    def _():
        xs_s[:, 0:pad, :] = jnp.zeros((nb, pad, d), F32)
        h_s[...] = jnp.zeros(h_s.shape, F32)

    cw = cw_ref[...]
    for s in range(nb):
        x = x_ref[s]
        xs_s[s, pad:pad + tc, :] = x
        xc = cb_ref[...] + cw[CONV_W - 1:CONV_W] * x
        for j in range(CONV_W - 1):
            off = pad - (CONV_W - 1) + j
            xc = xc + cw[j:j + 1] * xs_s[s, off:off + tc, :]
        xs_s[s, 0:pad, :] = x[tc - pad:tc]
        a, b = _rglru_coeffs(xc, wga_ref, bga_ref[...], wgx_ref, bgx_ref[...], lam_ref[...])
        a_s[s] = a
        b_s[s] = b

    def body(t, hs):
        row = pl.ds(t, 1)
        new = []
        for s in range(nb):
            h = a_s[s, row, :] * hs[s] + b_s[s, row, :]
            b_s[s, row, :] = h
            new.append(h)
        return tuple(new)

    hs = lax.fori_loop(0, tc, body, tuple(h_s[s] for s in range(nb)), unroll=8)
    for s in range(nb):
        h_s[s] = hs[s]
        hl_ref[s] = hs[s]
        y_ref[s] = (gate_ref[s] * b_s[s]).astype(y_ref.dtype)


def _rglru_prompt(x_br, gate, conv_w, conv_b, w_ga, b_ga, w_gx, b_gx, lam):
    b, t_len, d = x_br.shape
    tc = RNN_TIME_TILE if t_len % RNN_TIME_TILE == 0 else t_len
    nb = RNN_SEQS_PER_STEP if b % RNN_SEQS_PER_STEP == 0 else 1
    assert tc >= V7X_SUBLANES
    vec = lambda x: x.reshape(1, d)
    seq = pl.BlockSpec((nb, tc, d), lambda bi, ti: (bi, ti, 0))
    y, h_last = pl.pallas_call(
        _rglru_prompt_kernel,
        grid=(b // nb, t_len // tc),
        in_specs=[seq, seq, _const_spec(conv_w.shape), _const_spec((1, d)),
                  _const_spec(w_ga.shape), _const_spec((1, d)), _const_spec(w_gx.shape), _const_spec((1, d)),
                  _const_spec((1, d))],
        out_specs=[seq, pl.BlockSpec((nb, 1, d), lambda bi, ti: (bi, 0, 0))],
        out_shape=[jax.ShapeDtypeStruct((b, t_len, d), _MXU_DTYPE), jax.ShapeDtypeStruct((b, 1, d), F32)],
        scratch_shapes=[pltpu.VMEM((nb, V7X_SUBLANES + tc, d), F32), pltpu.VMEM((nb, 1, d), F32),
                        pltpu.VMEM((nb, tc, d), F32), pltpu.VMEM((nb, tc, d), F32)],
        compiler_params=_params("parallel", "arbitrary"),
        name="rglru_prompt",
    )(x_br, gate, conv_w, vec(conv_b), w_ga, vec(b_ga), w_gx, vec(b_gx), vec(lam))
    return y, h_last.reshape(b, d)


def _rglru_sample_kernel(x_ref, gate_ref, c0_ref, c1_ref, c2_ref, h0_ref, cw_ref, cb_ref,
                         wga_ref, bga_ref, wgx_ref, bgx_ref, lam_ref, y_ref, h_ref):
    cw = cw_ref[...]
    xc = cb_ref[...] + cw[0:1] * c0_ref[...] + cw[1:2] * c1_ref[...] + cw[2:3] * c2_ref[...] + cw[3:4] * x_ref[...]
    a, b = _rglru_coeffs(xc, wga_ref, bga_ref[...], wgx_ref, bgx_ref[...], lam_ref[...])
    h = a * h0_ref[...] + b
    h_ref[...] = h
    y_ref[...] = (gate_ref[...] * h).astype(y_ref.dtype)


def _rglru_sample(x_br, gate, conv_state, h0, conv_w, conv_b, w_ga, b_ga, w_gx, b_gx, lam):
    bd, d = x_br.shape
    vec = lambda x: x.reshape(1, d)
    args = (x_br, gate, conv_state[:, 0], conv_state[:, 1], conv_state[:, 2], h0, conv_w, vec(conv_b),
            w_ga, vec(b_ga), w_gx, vec(b_gx), vec(lam))
    return pl.pallas_call(
        _rglru_sample_kernel,
        grid=(1,),
        in_specs=[_const_spec(a.shape) for a in args],
        out_specs=[pl.BlockSpec((bd, d), lambda i: (0, 0))] * 2,
        out_shape=[jax.ShapeDtypeStruct((bd, d), _MXU_DTYPE), jax.ShapeDtypeStruct((bd, d), F32)],
        compiler_params=_params("arbitrary"),
        name="rglru_sample",
    )(*args)


def kernel(x_prompt, x_sample, cache_k, cache_v, state_gla, state_conv, state_rglru, page_table, rel_table, norm_g, w_up, w_down, w_in_even, w_alpha2, b_alpha2, lam_q1, lam_k1, lam_q2, lam_k2, g_diff, g_gla, w_out_even, w_in_odd, conv_w, conv_b, w_gate_a, b_gate_a, w_gate_x, b_gate_x, rg_lambda, w_out_odd):
    bp, t_len, d = x_prompt.shape
    bd, dec_seq, _ = x_sample.shape
    assert dec_seq == 1, "the sample group decodes one token per sequence"
    depth = norm_g.shape[0]
    yp = x_prompt.reshape(bp * t_len, d)
    ys = x_sample.reshape(bd, d)
    offs = [int(o) for o in np.cumsum(SPLIT_SIZES)]
    outs = {k: [] for k in ("kp", "vp", "ks", "vs", "gp", "gs", "cp", "cs", "rp", "rs")}

    for layer in range(depth):
        g4 = norm_g[layer]
        li = layer // 2
        wu = _mx(w_up[layer])
        wd = _mx(w_down[layer])
        if layer % 2 == 0:
            lam_init = 0.8 - 0.6 * math.exp(-0.3 * layer)
            w_in = _mx(w_in_even[li])
            w_parts = [w_in[:, a:b] for a, b in zip([0] + offs[:-1], offs)]
            w_a2 = _mx(w_alpha2[li])
            lam_vecs = jnp.stack([lam_q1[li], lam_k1[li], lam_q2[li], lam_k2[li]]).astype(F32)
            w_out = _mx(w_out_even[li])
            wo_parts = [w_out[:DA_V], w_out[DA_V:]]

            q, k, v, gq, gk, gv, la, og = _inproj_even(yp, g4[0], w_parts, w_a2, b_alpha2[li])
            bias = _prompt_bias(rel_table, t_len)
            sh = lambda x: x.reshape(bp, -1, x.shape[-1])
            a_out = _attn_prompt(sh(q), sh(k), sh(v), lam_vecs, bias, g_diff[li], lam_init)
            o_gla, s_p = _gla_prompt(sh(gq), sh(gk), sh(gv), sh(la), sh(og), g_gla[li])
            yp = _post([a_out.reshape(bp * t_len, DA_V), o_gla.reshape(bp * t_len, GLA_V)], wo_parts, yp, g4, wu, wd)
            outs["kp"].append(k.reshape(bp, t_len, DA_HEADS, 2 * DA_DH))
            outs["vp"].append(v.reshape(bp, t_len, DA_HEADS, DA_VD))
            outs["gp"].append(s_p)

            q, k, v, gq, gk, gv, la, og = _inproj_even(ys, g4[0], w_parts, w_a2, b_alpha2[li])
            a_out = _attn_sample(q, k, v, cache_k, cache_v, li, page_table, lam_vecs, rel_table, g_diff[li], lam_init)
            o_gla, s_s = _gla_sample(gq, gk, gv, la, og, state_gla[li], g_gla[li])
            ys = _post([a_out, o_gla], wo_parts, ys, g4, wu, wd)
            outs["ks"].append(k.reshape(bd, 1, DA_HEADS, 2 * DA_DH))
            outs["vs"].append(v.reshape(bd, 1, DA_HEADS, DA_VD))
            outs["gs"].append(s_s)
        else:
            w_in = _mx(w_in_odd[li])
            d_rnn = w_in.shape[1] // 2
            w_gate, w_x = w_in[:, :d_rnn], w_in[:, d_rnn:]
            wga, wgx = _mx(w_gate_a[li]), _mx(w_gate_x[li])
            w_out = [_mx(w_out_odd[li])]
            rnn = (conv_w[li], conv_b[li], wga, b_gate_a[li], wgx, b_gate_x[li], rg_lambda[li])

            gate, x_br = _inproj_odd(yp, g4[0], w_gate, w_x)
            x_br3 = x_br.reshape(bp, t_len, d_rnn)
            y, h_last = _rglru_prompt(x_br3, gate.reshape(bp, t_len, d_rnn), *rnn)
            yp = _post([y.reshape(bp * t_len, d_rnn)], w_out, yp, g4, wu, wd)
            assert t_len >= CONV_W - 1
            outs["cp"].append(x_br3[:, t_len - (CONV_W - 1):])
            outs["rp"].append(h_last)

            gate, x_br = _inproj_odd(ys, g4[0], w_gate, w_x)
            y, h_new = _rglru_sample(x_br, gate, state_conv[li], state_rglru[li], *rnn)
            ys = _post([y], w_out, ys, g4, wu, wd)
            outs["cs"].append(jnp.concatenate([state_conv[li][:, 1:], x_br[:, None, :]], axis=1))
            outs["rs"].append(h_new)

    st = lambda key: jnp.stack(outs[key])
    return (yp.reshape(bp, t_len, d), ys.reshape(bd, 1, d), st("kp"), st("vp"), st("ks"), st("vs"),
            st("gp"), st("gs"), st("cp"), st("cs"), st("rp"), st("rs"))
```

```python
import functools
import math

import numpy as np
import jax
import jax.numpy as jnp
from jax import lax
from jax.experimental import pallas as pl
from jax.experimental.pallas import tpu as pltpu

DA_HEADS = 4
DA_DH = 64
DA_VD = 2 * DA_DH
GLA_HEADS = 4
GLA_DK = 64
GLA_DV = 128
GLA_RANK = 16
GLA_TAU = 16.0
DA_QK = DA_HEADS * 2 * DA_DH
DA_V = DA_HEADS * DA_VD
GLA_QK = GLA_HEADS * GLA_DK
GLA_V = GLA_HEADS * GLA_DV
SPLIT_SIZES = (DA_QK, DA_QK, DA_V, GLA_QK, GLA_QK, GLA_V, GLA_RANK, GLA_V)
RNN_BLOCKS = 8
CONV_W = 4
RG_C = 8.0
NUM_BUCKETS = 32
MAX_DISTANCE = 128
PAGE_SIZE = 128
EPS = 1e-6
NEG = -1e30
LOG2E = math.log2(math.e)

V7X_LANES = 128
V7X_SUBLANES = 8
V7X_VMEM_LIMIT_BYTES = 56 * 1024 * 1024

F32 = jnp.float32
_MXU_DTYPE = jnp.bfloat16

ROW_TILE = 512
INPROJ_ROW_TILE = 1024
FF_CHUNK = 1024
ATTN_Q_TILE = 256
ATTN_K_TILE = 128
ATTN_HEADS_PER_STEP = 4
GLA_CHUNK = 64
GLA_SAMPLE_SEQS_PER_STEP = 8
GLA_UNROLL = 8
RNN_TIME_TILE = 128
RNN_SEQS_PER_STEP = 4
PAGES_PER_STEP = 32


def _mx(x):
    return x.astype(_MXU_DTYPE)


def _dot(a, b):
    return jnp.dot(_mx(a), _mx(b), preferred_element_type=F32)


def _dot_nt(a, b):
    return lax.dot_general(_mx(a), _mx(b), (((1,), (1,)), ((), ())), preferred_element_type=F32)


def _dot_tn(a, b):
    return lax.dot_general(_mx(a), _mx(b), (((0,), (0,)), ((), ())), preferred_element_type=F32)


def _rms(x, g):
    return x * lax.rsqrt(jnp.mean(x * x, axis=-1, keepdims=True) + EPS) * g


def _softplus(x):
    return jnp.maximum(x, 0.0) + jnp.log1p(jnp.exp(-jnp.abs(x)))


def _log_sigmoid(x):
    return -_softplus(-x)


def _sigmoid_tanh(x):
    return 0.5 * jnp.tanh(0.5 * x) + 0.5


def _gelu_tanh(x):
    c = math.sqrt(2.0 / math.pi)
    return 0.5 * x * (1.0 + jnp.tanh(c * (x + 0.044715 * (x * x * x))))


def _params(*sem):
    return pltpu.CompilerParams(dimension_semantics=sem, vmem_limit_bytes=V7X_VMEM_LIMIT_BYTES)


def _const_spec(shape):
    nd = len(shape)
    return pl.BlockSpec(shape, lambda *_: (0,) * nd, pipeline_mode=pl.Buffered(1))


def _row_tile(m, tile=None):
    tile = ROW_TILE if tile is None else tile
    return tile if m % tile == 0 else m


def _inproj_even_kernel(x_ref, g_ref, wq, wk, wv, wgq, wgk, wgv, wlr, wog, wa2, ba2,
                        q_o, k_o, v_o, gq_o, gk_o, gv_o, la_o, og_o):
    h = _mx(_rms(x_ref[...], g_ref[...]))
    q_o[...] = (_dot(h, wq[...]) * (DA_DH ** -0.5 * LOG2E)).astype(q_o.dtype)
    k = _dot(h, wk[...])
    v = _dot(h, wv[...])
    tm = k.shape[0]
    for hh in range(DA_HEADS):
        k_o[pl.ds(hh, tm, stride=DA_HEADS), :] = k[:, hh * 2 * DA_DH:(hh + 1) * 2 * DA_DH]
        v_o[pl.ds(hh, tm, stride=DA_HEADS), :] = v[:, hh * DA_VD:(hh + 1) * DA_VD]
    gq_o[...] = _dot(h, wgq[...]) * (GLA_DK ** -0.5)
    gk_o[...] = _dot(h, wgk[...])
    gv_o[...] = _dot(h, wgv[...]).astype(gv_o.dtype)
    z = _dot(_dot(h, wlr[...]), wa2[...]) + ba2[...]
    la_o[...] = _log_sigmoid(z) * (1.0 / GLA_TAU)
    og = _dot(h, wog[...])
    og_o[...] = og * jax.nn.sigmoid(og)


def _inproj_even(x, g, w_parts, w_a2, b_a2):
    m, d = x.shape
    tm = _row_tile(m, INPROJ_ROW_TILE)
    out_cols = (DA_QK, DA_QK, DA_V, GLA_QK, GLA_QK, GLA_V, GLA_QK, GLA_V)
    out_dtypes = (_MXU_DTYPE, F32, F32, F32, F32, _MXU_DTYPE, F32, F32)
    in_specs = [pl.BlockSpec((tm, d), lambda i: (i, 0)), _const_spec((1, d))]
    in_specs += [_const_spec(w.shape) for w in w_parts]
    in_specs += [_const_spec(w_a2.shape), _const_spec((1, GLA_QK))]
    return pl.pallas_call(
        _inproj_even_kernel,
        grid=(m // tm,),
        in_specs=in_specs,
        out_specs=[pl.BlockSpec((tm, n), lambda i: (i, 0)) if idx not in (1, 2)
                   else pl.BlockSpec((tm * DA_HEADS, n // DA_HEADS), lambda i: (i, 0))
                   for idx, n in enumerate(out_cols)],
        out_shape=[jax.ShapeDtypeStruct((m, n) if idx not in (1, 2) else (m * DA_HEADS, n // DA_HEADS), dt)
                   for idx, (n, dt) in enumerate(zip(out_cols, out_dtypes))],
        compiler_params=_params("parallel"),
        name="inproj_even",
    )(x, g.reshape(1, d), *w_parts, w_a2, b_a2.reshape(1, GLA_QK))


def _inproj_odd_kernel(x_ref, g_ref, wg, wx, gate_o, x_o):
    h = _mx(_rms(x_ref[...], g_ref[...]))
    gate_o[...] = _gelu_tanh(_dot(h, wg[...]))
    x_o[...] = _dot(h, wx[...])


def _inproj_odd(x, g, w_gate, w_x):
    m, d = x.shape
    tm = _row_tile(m, INPROJ_ROW_TILE)
    n = w_gate.shape[1]
    out_spec = pl.BlockSpec((tm, n), lambda i: (i, 0))
    out_shape = jax.ShapeDtypeStruct((m, n), F32)
    return pl.pallas_call(
        _inproj_odd_kernel,
        grid=(m // tm,),
        in_specs=[pl.BlockSpec((tm, d), lambda i: (i, 0)), _const_spec((1, d)),
                  _const_spec(w_gate.shape), _const_spec(w_x.shape)],
        out_specs=[out_spec] * 2,
        out_shape=[out_shape] * 2,
        compiler_params=_params("parallel"),
        name="inproj_odd",
    )(x, g.reshape(1, d), w_gate, w_x)


def _post_kernel(*refs, n_mix):
    mix_refs = refs[:n_mix]
    wo_refs = refs[n_mix:2 * n_mix]
    x_ref, g_ref, wup_ref, wdn_ref, y_ref = refs[2 * n_mix:]
    m = _dot(mix_refs[0][...], wo_refs[0][...])
    for mr, wr in zip(mix_refs[1:], wo_refs[1:]):
        m = m + _dot(mr[...], wr[...])
    g = g_ref[...]
    y1 = x_ref[...] + _rms(m, g[1:2])
    h = _mx(_rms(y1, g[2:3]))
    d_ff = wup_ref.shape[1]
    fc = min(FF_CHUNK, d_ff)
    acc = None
    for c in range(d_ff // fc):
        a = jnp.square(jnp.maximum(_dot(h, wup_ref[:, c * fc:(c + 1) * fc]), 0.0))
        part = _dot(a, wdn_ref[c * fc:(c + 1) * fc, :])
        acc = part if acc is None else acc + part
    y_ref[...] = y1 + _rms(acc, g[3:4])


def _post(mix_parts, wo_parts, x, g4, w_up, w_down):
    m, d = x.shape
    tm = _row_tile(m)
    n_mix = len(mix_parts)
    in_specs = [pl.BlockSpec((tm, p.shape[1]), lambda i: (i, 0)) for p in mix_parts]
    in_specs += [_const_spec(w.shape) for w in wo_parts]
    in_specs += [pl.BlockSpec((tm, d), lambda i: (i, 0)), _const_spec(g4.shape),
                 _const_spec(w_up.shape), _const_spec(w_down.shape)]
    return pl.pallas_call(
        functools.partial(_post_kernel, n_mix=n_mix),
        grid=(m // tm,),
        in_specs=in_specs,
        out_specs=pl.BlockSpec((tm, d), lambda i: (i, 0)),
        out_shape=jax.ShapeDtypeStruct((m, d), F32),
        compiler_params=_params("parallel"),
        name="post_mlp",
    )(*mix_parts, *wo_parts, x, g4, w_up, w_down)


def _t5_bucket(rel):
    rel = jnp.maximum(rel, 0)
    max_exact = NUM_BUCKETS // 2
    large = max_exact + (jnp.log(jnp.maximum(rel, 1).astype(F32) / max_exact)
                         / math.log(MAX_DISTANCE / max_exact) * (NUM_BUCKETS - max_exact)).astype(jnp.int32)
    large = jnp.minimum(large, NUM_BUCKETS - 1)
    return jnp.where(rel < max_exact, rel, large)


def _rel_bias(table, qpos, kpos):
    bucket = _t5_bucket(qpos[:, None] - kpos[None, :])
    return jnp.transpose(table[bucket], (2, 0, 1)).astype(F32)


def _check_far_bucket_constant(first, last):
    rel = np.arange(first, last + 1, dtype=np.float64)
    max_exact = NUM_BUCKETS // 2
    steps = np.log(rel / max_exact) / math.log(MAX_DISTANCE / max_exact) * (NUM_BUCKETS - max_exact)
    assert first >= max_exact and np.all(steps >= NUM_BUCKETS - max_exact - 0.5), "far keys must share one bias bucket"


def _lambda_from(lam_ref, lam_init):
    lv = lam_ref[...]
    s1 = jnp.sum(lv[0:1] * lv[1:2], axis=-1, keepdims=True)
    s2 = jnp.sum(lv[2:3] * lv[3:4], axis=-1, keepdims=True)
    return jnp.exp(s1) - jnp.exp(s2) + lam_init


def _attn_prompt_kernel(lam_ref, q_ref, k_ref, v_ref, bias_ref, g_ref, o_ref,
                        kb, vt, m_s, l_s, acc_s, *, tq, tk, nh, lam_init):
    i = pl.program_id(2)
    n_blk = vt.shape[0]
    hd = 2 * DA_DH
    per_q = tq // tk

    @pl.when(i == 0)
    def _():
        def prep(jj, carry):
            for h in range(nh):
                src = pl.ds(jj * (tk * nh) + h, tk, stride=nh)
                kb[pl.ds(pl.multiple_of(jj * tk, tk), tk), h * hd:(h + 1) * hd] = _mx(k_ref[0, src, :])
                vt[jj, h] = _mx(v_ref[0, src, :].T)
            return carry

        lax.fori_loop(0, n_blk, prep, 0)

    sub = lax.broadcasted_iota(jnp.int32, (hd, tq), 0)
    zero = jnp.zeros((hd, tq), F32)
    q_t = []
    for h in range(nh):
        qh = q_ref[0, :, h * hd:(h + 1) * hd].astype(F32).T
        q_t.append((_mx(jnp.where(sub < DA_DH, qh, zero)), _mx(jnp.where(sub >= DA_DH, qh, zero))))

    m_s[...] = jnp.full(m_s.shape, -jnp.inf, F32)
    l_s[...] = jnp.zeros(l_s.shape, F32)
    acc_s[...] = jnp.zeros(acc_s.shape, F32)

    def step(j, tile, q_lo=0):
        rows = pl.ds(pl.multiple_of(j * tk, tk), tk)
        qs = slice(q_lo, tq)
        for h in range(nh):
            k_blk = kb[rows, h * hd:(h + 1) * hd]
            v_blk = vt[j, h]
            for c in range(2):
                s = _dot(k_blk, q_t[h][c][:, qs])
                if tile is not None:
                    s = s + bias_ref[h, tile, :, qs]
                m_old = m_s[h, c, :, qs]
                m_new = jnp.maximum(m_old, jnp.max(s, axis=0, keepdims=True))
                alpha = jnp.exp2(m_old - m_new)
                p = jnp.exp2(s - m_new)
                l_s[h, c, :, qs] = alpha * l_s[h, c, :, qs] + jnp.sum(p, axis=0, keepdims=True)
                acc_s[h, c, :, qs] = alpha * acc_s[h, c, :, qs] + _dot(v_blk, p)
                m_s[h, c, :, qs] = m_new

    def far(jq, carry):
        for d in range(per_q):
            step(jq * per_q + d, None)
        return carry

    lax.fori_loop(0, jnp.maximum(i - 1, 0), far, 0)

    @pl.when(i > 0)
    def _():
        for d in range(per_q - 1):
            step((i - 1) * per_q + d, None)
        step(i * per_q - 1, 0)

    for d in range(per_q):
        step(i * per_q + d, 1 + d, q_lo=d * tk)

    lam = _lambda_from(lam_ref, lam_init)
    for h in range(nh):
        o = (acc_s[h, 0] / l_s[h, 0] - lam * (acc_s[h, 1] / l_s[h, 1])).T
        o_ref[0, :, h * DA_VD:(h + 1) * DA_VD] = (_rms(o, g_ref[...]) * (1.0 - lam_init)).astype(o_ref.dtype)


def _attn_prompt(q, k, v, lam_vecs, bias, g_da, lam_init):
    b, t_len, _ = q.shape
    tq, tk = _attn_tiles(t_len)
    nh = ATTN_HEADS_PER_STEP
    assert nh == DA_HEADS, "k / v arrive as (token, head) rows holding every head"
    hd = 2 * DA_DH
    return pl.pallas_call(
        functools.partial(_attn_prompt_kernel, tq=tq, tk=tk, nh=nh, lam_init=lam_init),
        grid=(b, DA_HEADS // nh, t_len // tq),
        in_specs=[
            _const_spec(lam_vecs.shape),
            pl.BlockSpec((1, tq, nh * hd), lambda bi, h, i: (bi, i, h)),
            pl.BlockSpec((1, t_len * nh, hd), lambda bi, h, i: (bi, 0, 0)),
            pl.BlockSpec((1, t_len * nh, DA_VD), lambda bi, h, i: (bi, 0, 0)),
            pl.BlockSpec((nh,) + bias.shape[1:], lambda bi, h, i: (h, 0, 0, 0)),
            _const_spec((1, DA_VD)),
        ],
        out_specs=pl.BlockSpec((1, tq, nh * DA_VD), lambda bi, h, i: (bi, i, h)),
        out_shape=jax.ShapeDtypeStruct((b, t_len, DA_V), _MXU_DTYPE),
        scratch_shapes=[
            pltpu.VMEM((t_len, nh * hd), _MXU_DTYPE),
            pltpu.VMEM((t_len // tk, nh, DA_VD, tk), _MXU_DTYPE),
            pltpu.VMEM((nh, 2, 1, tq), F32),
            pltpu.VMEM((nh, 2, 1, tq), F32),
            pltpu.VMEM((nh, 2, DA_VD, tq), F32),
        ],
        compiler_params=_params("parallel", "parallel", "arbitrary"),
        name="attn_prompt",
    )(lam_vecs, q, k, v, bias, g_da.reshape(1, DA_VD))


def _attn_tiles(t_len):
    tq = min(ATTN_Q_TILE, t_len)
    tk = min(ATTN_K_TILE, tq)
    assert t_len % tq == 0 and tq % tk == 0
    return tq, tk


def _prompt_bias(rel_table, t_len):
    tq, tk = _attn_tiles(t_len)
    if t_len > tq + tk:
        _check_far_bucket_constant(tk + 1, t_len)
    span = tk + tq
    period = span + tq
    idx = jnp.arange(period)
    dist = jnp.where(idx < span, tk - idx, tk + (period - idx))
    vec = rel_table[_t5_bucket(dist)].astype(F32) - rel_table[NUM_BUCKETS - 1].astype(F32)[None, :]
    vec = jnp.where((dist >= 0)[:, None], vec * LOG2E, NEG).T
    flat = jnp.tile(vec, (1, tq))[:, :tq * (period - 1)]
    bias = flat.reshape(vec.shape[0], tq, period - 1)[:, :, :span]
    return jnp.swapaxes(bias, 1, 2).reshape(bias.shape[0], 1 + tq // tk, tk, tq)


def _attn_sample_kernel(pt_ref, lam_ref, q_ref, kn_ref, vn_ref, bias_ref, biasn_ref, g_ref, *rest,
                        n_pg, lam_init):
    del pt_ref
    k_refs = rest[:n_pg]
    v_refs = rest[n_pg:2 * n_pg]
    o_ref, m_s, l_s, acc_s = rest[2 * n_pg:]
    step_i = pl.program_id(1)
    n_steps = pl.num_programs(1)
    rows8 = 2 * DA_HEADS
    hd = 2 * DA_DH
    pg_cols = PAGE_SIZE * DA_HEADS // 2

    @pl.when(step_i == 0)
    def _():
        m_s[...] = jnp.full(m_s.shape, -jnp.inf, F32)
        l_s[...] = jnp.zeros(l_s.shape, F32)
        acc_s[...] = jnp.zeros(acc_s.shape, F32)

    def rep2(x):
        return jnp.concatenate([x[r // 2:r // 2 + 1] for r in range(rows8)], axis=0)

    def side_by_side(x):
        even = (lax.broadcasted_iota(jnp.int32, x.shape, 0) // 2) % 2 == 0
        return jnp.concatenate([jnp.where(even, x, 0.0), jnp.where(even, 0.0, x)], axis=1)

    def page_pairs(ref):
        return jnp.concatenate([ref[0, pl.ds(0, pg_cols, stride=2), :], ref[0, pl.ds(1, pg_cols, stride=2), :]], axis=1)

    rowi = lax.broadcasted_iota(jnp.int32, (rows8, hd), 0)
    lanei = lax.broadcasted_iota(jnp.int32, (rows8, hd), 1)
    q8 = side_by_side(jnp.where(lanei // DA_DH == rowi % 2, rep2(q_ref[0].astype(F32)), 0.0))

    s = jnp.concatenate([_dot_nt(q8, page_pairs(kr)) for kr in k_refs], axis=1) + bias_ref[...]
    m_old = m_s[...]
    m_new = jnp.maximum(m_old, jnp.max(s, axis=-1, keepdims=True))
    alpha = jnp.exp2(m_old - m_new)
    p = jnp.exp2(s - m_new)
    l_new = alpha * l_s[...] + jnp.sum(p, axis=-1, keepdims=True)
    pv = _dot(p[:, 0:pg_cols], page_pairs(v_refs[0]))
    for gi in range(1, n_pg):
        pv = pv + _dot(p[:, gi * pg_cols:(gi + 1) * pg_cols], page_pairs(v_refs[gi]))
    acc_new = alpha * acc_s[...] + pv
    m_s[...] = m_new
    l_s[...] = l_new
    acc_s[...] = acc_new

    @pl.when(step_i == n_steps - 1)
    def _():
        kn = rep2(kn_ref[0])
        vn = rep2(vn_ref[0])
        sn = jnp.sum(q8 * jnp.concatenate([kn, kn], axis=1), axis=-1, keepdims=True) + biasn_ref[...]
        m_fin = jnp.maximum(m_new, sn)
        a2 = jnp.exp2(m_new - m_fin)
        pn = jnp.exp2(sn - m_fin)
        l_fin = a2 * l_new + pn
        acc_fin = a2 * acc_new + pn * jnp.concatenate([vn, vn], axis=1)
        an = acc_fin / l_fin
        lam = _lambda_from(lam_ref, lam_init)
        own = [slice((h % 2) * DA_VD, (h % 2 + 1) * DA_VD) for h in range(DA_HEADS)]
        o = jnp.concatenate([an[2 * h:2 * h + 1, own[h]] - lam * an[2 * h + 1:2 * h + 2, own[h]]
                             for h in range(DA_HEADS)], axis=0)
        o_ref[0] = (_rms(o, g_ref[...]) * (1.0 - lam_init)).astype(o_ref.dtype)


def _attn_sample(q, k_new, v_new, cache_k, cache_v, layer_idx, page_table, lam_vecs, rel_table, g_da, lam_init):
    bd = q.shape[0]
    n_pages = page_table.shape[1]
    past = n_pages * PAGE_SIZE
    n_pool = cache_k.shape[1]
    hd = 2 * DA_DH
    pg_rows = PAGE_SIZE * DA_HEADS
    n_pg = PAGES_PER_STEP if n_pages % PAGES_PER_STEP == 0 else 1
    n_steps = n_pages // n_pg
    ck = cache_k.reshape(cache_k.shape[0] * n_pool, pg_rows, hd)
    cv = cache_v.reshape(cache_v.shape[0] * n_pool, pg_rows, DA_VD)
    base = layer_idx * n_pool

    qpos = jnp.full((1,), past, jnp.int32)
    n_near = min(past, MAX_DISTANCE)
    bias_p = _rel_bias(rel_table, qpos, jnp.arange(past - n_near, past))[:, 0, :]
    if past > n_near:
        _check_far_bucket_constant(n_near + 1, past)
        far = jnp.broadcast_to(rel_table[NUM_BUCKETS - 1].astype(F32)[:, None], (DA_HEADS, past - n_near))
        bias_p = jnp.concatenate([far, bias_p], axis=1)
    n_pair = DA_HEADS // 2
    col_pair = jnp.arange(past * n_pair) % n_pair
    own_pair = col_pair[None, :] == (jnp.arange(DA_HEADS) // 2)[:, None]
    bias_p = jnp.where(own_pair, jnp.repeat(bias_p * LOG2E, n_pair, axis=1), NEG)
    bias_p = jnp.repeat(bias_p, 2, axis=0)
    bias_n = jnp.repeat(_rel_bias(rel_table, qpos, qpos)[:, 0, :] * LOG2E, 2, axis=0)

    def page_spec(gi):
        return pl.BlockSpec((1, pg_rows, hd), lambda b, s, pt: (base + pt[b, s * n_pg + gi], 0, 0))

    row3 = lambda b, s, pt: (b, 0, 0)
    const2 = lambda b, s, pt: (0, 0)
    in_specs = [
        pl.BlockSpec(lam_vecs.shape, const2),
        pl.BlockSpec((1, DA_HEADS, hd), row3),
        pl.BlockSpec((1, DA_HEADS, hd), row3),
        pl.BlockSpec((1, DA_HEADS, DA_VD), row3),
        pl.BlockSpec((2 * DA_HEADS, n_pg * pg_rows // 2), lambda b, s, pt: (0, s)),
        pl.BlockSpec((2 * DA_HEADS, 1), const2),
        pl.BlockSpec((1, DA_VD), const2),
    ]
    in_specs += [page_spec(gi) for gi in range(n_pg)] + [page_spec(gi) for gi in range(n_pg)]
    out = pl.pallas_call(
        functools.partial(_attn_sample_kernel, n_pg=n_pg, lam_init=lam_init),
        grid_spec=pltpu.PrefetchScalarGridSpec(
            num_scalar_prefetch=1,
            grid=(bd, n_steps),
            in_specs=in_specs,
            out_specs=pl.BlockSpec((1, DA_HEADS, DA_VD), row3),
            scratch_shapes=[
                pltpu.VMEM((2 * DA_HEADS, 1), F32),
                pltpu.VMEM((2 * DA_HEADS, 1), F32),
                pltpu.VMEM((2 * DA_HEADS, 2 * DA_VD), F32),
            ],
        ),
        out_shape=jax.ShapeDtypeStruct((bd, DA_HEADS, DA_VD), _MXU_DTYPE),
        compiler_params=_params("parallel", "arbitrary"),
        name="attn_sample",
    )(page_table, lam_vecs, q.reshape(bd, DA_HEADS, hd), k_new.reshape(bd, DA_HEADS, hd),
      v_new.reshape(bd, DA_HEADS, DA_VD), bias_p, bias_n, g_da.reshape(1, DA_VD), *([ck] * n_pg), *([cv] * n_pg))
    return out.reshape(bd, DA_V)


def _gla_prompt_kernel(q_ref, k_ref, v_ref, la_ref, og_ref, g_ref, o_ref, s_ref, st_s, *, chunk):
    t_len = q_ref.shape[1]
    c_sz = chunk
    pair_w = 2 * GLA_DK
    st_s[...] = jnp.zeros(st_s.shape, F32)
    r_i = lax.broadcasted_iota(jnp.int32, (c_sz, c_sz), 0)
    c_i = lax.broadcasted_iota(jnp.int32, (c_sz, c_sz), 1)
    tril = r_i >= c_i
    tril_f = tril.astype(F32)
    tril2 = jnp.concatenate([tril, tril], axis=0)
    lane = lax.broadcasted_iota(jnp.int32, (c_sz, pair_w), 1)
    first = lane < GLA_DK
    g = g_ref[...]

    def split_heads(x):
        zero = jnp.zeros_like(x)
        return jnp.concatenate([jnp.where(first, x, zero), jnp.where(first, zero, x)], axis=0)

    def body(c, carry):
        rows = pl.ds(pl.multiple_of(c * c_sz, c_sz), c_sz)
        la = la_ref[0, rows, :]
        bc = jnp.dot(tril_f, la, precision=lax.Precision.HIGHEST, preferred_element_type=F32)
        b_last = bc[c_sz - 1:c_sz, :]
        q = q_ref[0, rows, :]
        k = k_ref[0, rows, :]
        qe = q * jnp.exp(bc)
        ke = k * jnp.exp(-bc)
        kd = k * jnp.exp(b_last - bc)
        dec = jnp.exp(b_last)
        v = v_ref[0, rows, :]
        og = og_ref[0, rows, :]
        for p in range(GLA_HEADS // 2):
            ps = slice(p * pair_w, (p + 1) * pair_w)
            qs = _mx(split_heads(qe[:, ps]))
            att = jnp.where(tril2, _dot_nt(qs, ke[:, ps]), 0.0)
            st = st_s[p]
            inter = _dot_nt(qs, st)
            va = v[:, (2 * p) * GLA_DV:(2 * p + 1) * GLA_DV]
            vb = v[:, (2 * p + 1) * GLA_DV:(2 * p + 2) * GLA_DV]
            o_a = _dot(att[0:c_sz], va) + inter[0:c_sz]
            o_b = _dot(att[c_sz:2 * c_sz], vb) + inter[c_sz:2 * c_sz]
            upd = _dot_tn(jnp.concatenate([va, vb], axis=0), split_heads(kd[:, ps]))
            st_s[p] = dec[:, ps] * st + upd
            for hh, o_h in ((2 * p, o_a), (2 * p + 1, o_b)):
                cols = slice(hh * GLA_DV, (hh + 1) * GLA_DV)
                o_ref[0, rows, cols] = (_rms(o_h, g) * og[:, cols]).astype(o_ref.dtype)
        return carry

    lax.fori_loop(0, t_len // c_sz, body, 0, unroll=GLA_UNROLL)
    for p in range(GLA_HEADS // 2):
        s_t = st_s[p].T
        s_ref[0, 2 * p] = s_t[0:GLA_DK]
        s_ref[0, 2 * p + 1] = s_t[GLA_DK:2 * GLA_DK]


def _gla_prompt(gq, gk, gv, la, og, g_gla):
    b, t_len, _ = gq.shape
    chunk = GLA_CHUNK if t_len % GLA_CHUNK == 0 else t_len
    seq = lambda n: pl.BlockSpec((1, t_len, n), lambda bi: (bi, 0, 0))
    return pl.pallas_call(
        functools.partial(_gla_prompt_kernel, chunk=chunk),
        grid=(b,),
        in_specs=[seq(GLA_QK), seq(GLA_QK), seq(GLA_V), seq(GLA_QK), seq(GLA_V), _const_spec((1, GLA_DV))],
        out_specs=[seq(GLA_V), pl.BlockSpec((1, GLA_HEADS, GLA_DK, GLA_DV), lambda bi: (bi, 0, 0, 0))],
        out_shape=[jax.ShapeDtypeStruct((b, t_len, GLA_V), _MXU_DTYPE),
                   jax.ShapeDtypeStruct((b, GLA_HEADS, GLA_DK, GLA_DV), F32)],
        scratch_shapes=[pltpu.VMEM((GLA_HEADS // 2, GLA_DV, 2 * GLA_DK), F32)],
        compiler_params=_params("parallel"),
        name="gla_prompt",
    )(gq, gk, gv, la, og, g_gla.reshape(1, GLA_DV))


def _gla_sample_kernel(qc_ref, kc_ref, lac_ref, v_ref, og_ref, s0_ref, g_ref, o_ref, s_ref):
    g = g_ref[...]
    for bi in range(qc_ref.shape[0]):
        for h in range(GLA_HEADS):
            bc = lac_ref[bi, h]
            qe = qc_ref[bi, h] * jnp.exp(bc)
            ke = kc_ref[bi, h] * jnp.exp(-bc)
            kd = kc_ref[bi, h] * jnp.exp(bc - bc)
            v = v_ref[bi, h].astype(F32)
            s0 = s0_ref[bi, h]
            att = jnp.sum(qe * ke, axis=0, keepdims=True)
            o = att * v + jnp.sum(qe * s0, axis=0, keepdims=True)
            s_ref[bi, h] = jnp.exp(bc) * s0 + kd * v
            o_ref[bi, h] = (_rms(o, g) * og_ref[bi, h]).astype(o_ref.dtype)


def _gla_sample(gq, gk, gv, la, og, s0, g_gla):
    bd = gq.shape[0]
    nb = GLA_SAMPLE_SEQS_PER_STEP if bd % GLA_SAMPLE_SEQS_PER_STEP == 0 else 1
    col = lambda x: x.reshape(bd, GLA_HEADS, GLA_DK, 1)
    row = lambda x: x.reshape(bd, GLA_HEADS, 1, GLA_DV)
    col_spec = pl.BlockSpec((nb, GLA_HEADS, GLA_DK, 1), lambda b: (b, 0, 0, 0))
    row_spec = pl.BlockSpec((nb, GLA_HEADS, 1, GLA_DV), lambda b: (b, 0, 0, 0))
    st_spec = pl.BlockSpec((nb, GLA_HEADS, GLA_DK, GLA_DV), lambda b: (b, 0, 0, 0))
    o, s_new = pl.pallas_call(
        _gla_sample_kernel,
        grid=(bd // nb,),
        in_specs=[col_spec, col_spec, col_spec, row_spec, row_spec, st_spec, _const_spec((1, GLA_DV))],
        out_specs=[row_spec, st_spec],
        out_shape=[jax.ShapeDtypeStruct((bd, GLA_HEADS, 1, GLA_DV), _MXU_DTYPE),
                   jax.ShapeDtypeStruct((bd, GLA_HEADS, GLA_DK, GLA_DV), F32)],
        compiler_params=_params("parallel"),
        name="gla_sample",
    )(col(gq), col(gk), col(la), row(gv), row(og), s0, g_gla.reshape(1, GLA_DV))
    return o.reshape(bd, GLA_V), s_new


def _rglru_coeffs(xc, wga_ref, bga, wgx_ref, bgx, lam):
    bw = wga_ref.shape[1]
    r_parts, i_parts = [], []
    for n in range(wga_ref.shape[0]):
        xb = _mx(xc[:, n * bw:(n + 1) * bw])
        r_parts.append(_dot(xb, wga_ref[n]))
        i_parts.append(_dot(xb, wgx_ref[n]))
    r = _sigmoid_tanh(jnp.concatenate(r_parts, axis=1) + bga)
    i = _sigmoid_tanh(jnp.concatenate(i_parts, axis=1) + bgx)
    log_a = (-RG_C * _softplus(-lam)) * r
    a = jnp.exp(log_a)
    b = jnp.sqrt(-jnp.tanh(log_a) * (a * a + 1.0)) * (i * xc)
    return a, b


def _rglru_prompt_kernel(x_ref, gate_ref, cw_ref, cb_ref, wga_ref, bga_ref, wgx_ref, bgx_ref, lam_ref,
                         y_ref, hl_ref, xs_s, h_s, a_s, b_s):
    nb, tc, d = x_ref.shape
    pad = xs_s.shape[1] - tc

    @pl.when(pl.program_id(1) == 0)
    def _():
        xs_s[:, 0:pad, :] = jnp.zeros((nb, pad, d), F32)
        h_s[...] = jnp.zeros(h_s.shape, F32)

    cw = cw_ref[...]
    for s in range(nb):
        x = x_ref[s]
        xs_s[s, pad:pad + tc, :] = x
        xc = cb_ref[...] + cw[CONV_W - 1:CONV_W] * x
        for j in range(CONV_W - 1):
            off = pad - (CONV_W - 1) + j
            xc = xc + cw[j:j + 1] * xs_s[s, off:off + tc, :]
        xs_s[s, 0:pad, :] = x[tc - pad:tc]
        a, b = _rglru_coeffs(xc, wga_ref, bga_ref[...], wgx_ref, bgx_ref[...], lam_ref[...])
        a_s[s] = a
        b_s[s] = b

    def body(t, hs):
        row = pl.ds(t, 1)
        new = []
        for s in range(nb):
            h = a_s[s, row, :] * hs[s] + b_s[s, row, :]
            b_s[s, row, :] = h
            new.append(h)
        return tuple(new)

    hs = lax.fori_loop(0, tc, body, tuple(h_s[s] for s in range(nb)), unroll=8)
    for s in range(nb):
        h_s[s] = hs[s]
        hl_ref[s] = hs[s]
        y_ref[s] = (gate_ref[s] * b_s[s]).astype(y_ref.dtype)


def _rglru_prompt(x_br, gate, conv_w, conv_b, w_ga, b_ga, w_gx, b_gx, lam):
    b, t_len, d = x_br.shape
    tc = RNN_TIME_TILE if t_len % RNN_TIME_TILE == 0 else t_len
    nb = RNN_SEQS_PER_STEP if b % RNN_SEQS_PER_STEP == 0 else 1
    assert tc >= V7X_SUBLANES
    vec = lambda x: x.reshape(1, d)
    seq = pl.BlockSpec((nb, tc, d), lambda bi, ti: (bi, ti, 0))
    y, h_last = pl.pallas_call(
        _rglru_prompt_kernel,
        grid=(b // nb, t_len // tc),
        in_specs=[seq, seq, _const_spec(conv_w.shape), _const_spec((1, d)),
                  _const_spec(w_ga.shape), _const_spec((1, d)), _const_spec(w_gx.shape), _const_spec((1, d)),
                  _const_spec((1, d))],
        out_specs=[seq, pl.BlockSpec((nb, 1, d), lambda bi, ti: (bi, 0, 0))],
        out_shape=[jax.ShapeDtypeStruct((b, t_len, d), _MXU_DTYPE), jax.ShapeDtypeStruct((b, 1, d), F32)],
        scratch_shapes=[pltpu.VMEM((nb, V7X_SUBLANES + tc, d), F32), pltpu.VMEM((nb, 1, d), F32),
                        pltpu.VMEM((nb, tc, d), F32), pltpu.VMEM((nb, tc, d), F32)],
        compiler_params=_params("parallel", "arbitrary"),
        name="rglru_prompt",
    )(x_br, gate, conv_w, vec(conv_b), w_ga, vec(b_ga), w_gx, vec(b_gx), vec(lam))
    return y, h_last.reshape(b, d)


def _rglru_sample_kernel(x_ref, gate_ref, c0_ref, c1_ref, c2_ref, h0_ref, cw_ref, cb_ref,
                         wga_ref, bga_ref, wgx_ref, bgx_ref, lam_ref, y_ref, h_ref):
    cw = cw_ref[...]
    xc = cb_ref[...] + cw[0:1] * c0_ref[...] + cw[1:2] * c1_ref[...] + cw[2:3] * c2_ref[...] + cw[3:4] * x_ref[...]
    a, b = _rglru_coeffs(xc, wga_ref, bga_ref[...], wgx_ref, bgx_ref[...], lam_ref[...])
    h = a * h0_ref[...] + b
    h_ref[...] = h
    y_ref[...] = (gate_ref[...] * h).astype(y_ref.dtype)


def _rglru_sample(x_br, gate, conv_state, h0, conv_w, conv_b, w_ga, b_ga, w_gx, b_gx, lam):
    bd, d = x_br.shape
    vec = lambda x: x.reshape(1, d)
    args = (x_br, gate, conv_state[:, 0], conv_state[:, 1], conv_state[:, 2], h0, conv_w, vec(conv_b),
            w_ga, vec(b_ga), w_gx, vec(b_gx), vec(lam))
    return pl.pallas_call(
        _rglru_sample_kernel,
        grid=(1,),
        in_specs=[_const_spec(a.shape) for a in args],
        out_specs=[pl.BlockSpec((bd, d), lambda i: (0, 0))] * 2,
        out_shape=[jax.ShapeDtypeStruct((bd, d), _MXU_DTYPE), jax.ShapeDtypeStruct((bd, d), F32)],
        compiler_params=_params("arbitrary"),
        name="rglru_sample",
    )(*args)


def kernel(x_prompt, x_sample, cache_k, cache_v, state_gla, state_conv, state_rglru, page_table, rel_table, norm_g, w_up, w_down, w_in_even, w_alpha2, b_alpha2, lam_q1, lam_k1, lam_q2, lam_k2, g_diff, g_gla, w_out_even, w_in_odd, conv_w, conv_b, w_gate_a, b_gate_a, w_gate_x, b_gate_x, rg_lambda, w_out_odd):
    bp, t_len, d = x_prompt.shape
    bd, dec_seq, _ = x_sample.shape
    assert dec_seq == 1, "the sample group decodes one token per sequence"
    depth = norm_g.shape[0]
    yp = x_prompt.reshape(bp * t_len, d)
    ys = x_sample.reshape(bd, d)
    offs = [int(o) for o in np.cumsum(SPLIT_SIZES)]
    outs = {k: [] for k in ("kp", "vp", "ks", "vs", "gp", "gs", "cp", "cs", "rp", "rs")}

    for layer in range(depth):
        g4 = norm_g[layer]
        li = layer // 2
        wu = _mx(w_up[layer])
        wd = _mx(w_down[layer])
        if layer % 2 == 0:
            lam_init = 0.8 - 0.6 * math.exp(-0.3 * layer)
            w_in = _mx(w_in_even[li])
            w_parts = [w_in[:, a:b] for a, b in zip([0] + offs[:-1], offs)]
            w_a2 = _mx(w_alpha2[li])
            lam_vecs = jnp.stack([lam_q1[li], lam_k1[li], lam_q2[li], lam_k2[li]]).astype(F32)
            w_out = _mx(w_out_even[li])
            wo_parts = [w_out[:DA_V], w_out[DA_V:]]

            q, k, v, gq, gk, gv, la, og = _inproj_even(yp, g4[0], w_parts, w_a2, b_alpha2[li])
            bias = _prompt_bias(rel_table, t_len)
            sh = lambda x: x.reshape(bp, -1, x.shape[-1])
            a_out = _attn_prompt(sh(q), sh(k), sh(v), lam_vecs, bias, g_diff[li], lam_init)
            o_gla, s_p = _gla_prompt(sh(gq), sh(gk), sh(gv), sh(la), sh(og), g_gla[li])
            yp = _post([a_out.reshape(bp * t_len, DA_V), o_gla.reshape(bp * t_len, GLA_V)], wo_parts, yp, g4, wu, wd)
            outs["kp"].append(k.reshape(bp, t_len, DA_HEADS, 2 * DA_DH))
            outs["vp"].append(v.reshape(bp, t_len, DA_HEADS, DA_VD))
            outs["gp"].append(s_p)

            q, k, v, gq, gk, gv, la, og = _inproj_even(ys, g4[0], w_parts, w_a2, b_alpha2[li])
            a_out = _attn_sample(q, k, v, cache_k, cache_v, li, page_table, lam_vecs, rel_table, g_diff[li], lam_init)
            o_gla, s_s = _gla_sample(gq, gk, gv, la, og, state_gla[li], g_gla[li])
            ys = _post([a_out, o_gla], wo_parts, ys, g4, wu, wd)
            outs["ks"].append(k.reshape(bd, 1, DA_HEADS, 2 * DA_DH))
            outs["vs"].append(v.reshape(bd, 1, DA_HEADS, DA_VD))
            outs["gs"].append(s_s)
        else:
            w_in = _mx(w_in_odd[li])
            d_rnn = w_in.shape[1] // 2
            w_gate, w_x = w_in[:, :d_rnn], w_in[:, d_rnn:]
            wga, wgx = _mx(w_gate_a[li]), _mx(w_gate_x[li])
            w_out = [_mx(w_out_odd[li])]
            rnn = (conv_w[li], conv_b[li], wga, b_gate_a[li], wgx, b_gate_x[li], rg_lambda[li])

            gate, x_br = _inproj_odd(yp, g4[0], w_gate, w_x)
            x_br3 = x_br.reshape(bp, t_len, d_rnn)
            y, h_last = _rglru_prompt(x_br3, gate.reshape(bp, t_len, d_rnn), *rnn)
            yp = _post([y.reshape(bp * t_len, d_rnn)], w_out, yp, g4, wu, wd)
            assert t_len >= CONV_W - 1
            outs["cp"].append(x_br3[:, t_len - (CONV_W - 1):])
            outs["rp"].append(h_last)

            gate, x_br = _inproj_odd(ys, g4[0], w_gate, w_x)
            y, h_new = _rglru_sample(x_br, gate, state_conv[li], state_rglru[li], *rnn)
            ys = _post([y], w_out, ys, g4, wu, wd)
            outs["cs"].append(jnp.concatenate([state_conv[li][:, 1:], x_br[:, None, :]], axis=1))
            outs["rs"].append(h_new)

    st = lambda key: jnp.stack(outs[key])
    return (yp.reshape(bp, t_len, d), ys.reshape(bd, 1, d), st("kp"), st("vp"), st("ks"), st("vs"),
            st("gp"), st("gs"), st("cp"), st("cs"), st("rp"), st("rs"))
```
